```python
import math
import jax
import jax.numpy as jnp
from jax import lax
import numpy as np

D_MODEL = 1024
BATCH = 8
SEQ = 2048
DEPTH = 4
DEC_BATCH = 32
DEC_SEQ = 8
PAST_LEN = 8192
PAGE_SIZE = 128

D_MIX = 2 * D_MODEL
W_A = D_MIX // 4
W_B = D_MIX // 4
W_C = D_MIX // 4
W_D = D_MIX - W_A - W_B - W_C
HA = 4
DA = W_A // (2 * HA)
GB = 4
CG = W_B // GB
CHUNK = 128
HC = 4
DVC = W_C // HC
DKC = DVC // 2
HD = 8
PD = W_D // HD
NS = 128
NG = 2
CONV_W = 4
CONV_DIM = W_D + 2 * NG * NS
QBLK = 128
ALPHA = (2.0 * DEPTH) ** 0.25
BETA = (8.0 * DEPTH) ** -0.25
LN_EPS = 1e-5
NEG_INF = -1e30
SPLIT_SIZES = (W_A, W_A, W_A, W_A, W_B, W_B, W_B, HC * DKC, HC * DKC, W_C, W_C, W_D, CONV_DIM, HD)
SPLIT_POINTS = tuple(int(v) for v in np.cumsum(SPLIT_SIZES)[:-1])
IN_COLS = int(sum(SPLIT_SIZES))

kernel_name = 'hybrid_diffattn_gmlp_retnet_ssd_step'


def layer_norm(x, g, b):
    xf = x.astype(jnp.float32)
    xc = xf - jnp.mean(xf, -1, keepdims=True)
    var = jnp.mean(xc * xc, -1, keepdims=True)
    return xc * lax.rsqrt(var + LN_EPS) * g.astype(jnp.float32) + b.astype(jnp.float32)


def rms_norm(x, g):
    xf = x.astype(jnp.float32)
    return xf * lax.rsqrt(jnp.mean(xf * xf, -1, keepdims=True) + LN_EPS) * g.astype(jnp.float32)


def diff_attention(q, qpos, segments, slopes, lam):
    scale = DA ** -0.5
    logits = []
    for k, _, kpos in segments:
        kk = k.reshape(k.shape[:3] + (2, DA))
        s = jnp.einsum('bqhcd,bkhcd->bhcqk', q, kk).astype(jnp.float32) * scale
        dist = (qpos[:, None] - kpos[None, :]).astype(jnp.float32)
        s = s - slopes[:, None, None, None] * dist
        logits.append(jnp.where(dist >= 0, s, NEG_INF))
    p = jax.nn.softmax(jnp.concatenate(logits, axis=-1), axis=-1)
    pd = p[:, :, 0] - lam * p[:, :, 1]
    out = 0.0
    off = 0
    for _, v, kpos in segments:
        n = kpos.shape[0]
        out = out + jnp.einsum('bhqk,bkhe->bqhe', pd[..., off:off + n].astype(v.dtype), v)
        off += n
    return out


def retention_chunk(q, k, v, s0, log_g):
    q, k, v = (z.astype(jnp.float32) for z in (q, k, v))
    L = q.shape[1]
    n = jnp.arange(L, dtype=jnp.float32)
    diff = n[:, None] - n[None, :]
    decay = jnp.where(diff >= 0, jnp.exp(log_g[:, None, None] * jnp.maximum(diff, 0.0)), 0.0)
    scores = jnp.einsum('blhd,bshd->bhls', q, k) * decay
    o = jnp.einsum('bhls,bshe->blhe', scores, v)
    o = o + jnp.einsum('blhd,bhde->blhe', q, s0) * jnp.exp(log_g[None, :] * (n[:, None] + 1.0))[None, :, :, None]
    w_k = jnp.exp(log_g[None, :] * (L - 1.0 - n[:, None]))
    s = jnp.exp(log_g * L)[None, :, None, None] * s0 + jnp.einsum('blhd,blhe,lh->bhde', k, v, w_k)
    return o, s


def ssd_chunk(x, bm, cm, dt, a, h0):
    x, bm, cm = (z.astype(jnp.float32) for z in (x, bm, cm))
    L = x.shape[1]
    cum = jnp.cumsum(dt * a, axis=1)
    causal = jnp.tril(jnp.ones((L, L), dtype=bool))[None, :, :, None]
    seg = cum[:, :, None, :] - cum[:, None, :, :]
    lmat = jnp.where(causal, jnp.exp(jnp.where(causal, seg, 0.0)), 0.0)
    scores = jnp.einsum('blhn,bshn->blsh', cm, bm) * lmat
    y = jnp.einsum('blsh,bsh,bshp->blhp', scores, dt, x)
    y = y + jnp.einsum('blhn,bhpn->blhp', cm, h0) * jnp.exp(cum)[..., None]
    w_last = jnp.exp(cum[:, -1:] - cum) * dt
    h = jnp.exp(cum[:, -1])[:, :, None, None] * h0 + jnp.einsum('blh,blhp,blhn->bhpn', w_last, x, bm)
    return y, h


def scan_chunks(step, seqs, state0, chunk):
    b, t = seqs[0].shape[:2]
    nc = t // chunk
    xs = tuple(jnp.moveaxis(s.reshape((b, nc, chunk) + s.shape[2:]), 1, 0) for s in seqs)

    def body(state, xc):
        y, state = step(*xc, state)
        return state, y

    state, ys = lax.scan(body, state0, xs)
    ys = jnp.moveaxis(ys, 0, 1).reshape((b, t) + ys.shape[3:])
    return ys, state


def trunk_layer(x, l, w_in, w_out, ln_g, ln_b, lam_q1, lam_k1, lam_q2, lam_k2, subln_g,
                gmlp_ln_g, gmlp_ln_b, gmlp_ws, gmlp_bs, ret_gn_g, ret_gn_b, conv_w, conv_b,
                dt_bias, a_log, d_skip, ssd_norm_g, attn_past, ret_s0, conv_buf, ssm_h0, pos0):
    f32 = jnp.float32
    b, t, _ = x.shape
    proj = jnp.einsum('btd,dc->btc', x, w_in)
    (qa, ka, va, ga, ub, vb, gb, qc, kc, vc, gc, zd, xbc, dtd) = jnp.split(proj, SPLIT_POINTS, axis=-1)
    qpos = pos0 + jnp.arange(t)
    L = min(t, CHUNK)

    lam_init = 0.8 - 0.6 * math.exp(-0.3 * l)
    lam = (jnp.exp(jnp.sum(lam_q1.astype(f32) * lam_k1.astype(f32)))
           - jnp.exp(jnp.sum(lam_q2.astype(f32) * lam_k2.astype(f32))) + lam_init)
    slopes = 2.0 ** (-8.0 * jnp.arange(1, HA + 1, dtype=f32) / HA)
    qa = qa.reshape(b, t, HA, 2, DA)
    ka = ka.reshape(b, t, HA, 2 * DA)
    va = va.reshape(b, t, HA, 2 * DA)
    if attn_past is None:
        nb = t // QBLK
        qb = jnp.moveaxis(qa.reshape(b, nb, QBLK, HA, 2, DA), 1, 0)
        segs = [(ka, va, qpos)]
        ob = lax.map(lambda args: diff_attention(args[0], args[1], segs, slopes, lam),
                     (qb, qpos.reshape(nb, QBLK)))
        oa = jnp.moveaxis(ob, 0, 1).reshape(b, t, HA, 2 * DA)
    else:
        k_past, v_past = attn_past
        past_pos = jnp.arange(k_past.shape[1])
        oa = diff_attention(qa, qpos, [(k_past, v_past, past_pos), (ka, va, qpos)], slopes, lam)
    oa = rms_norm(oa, subln_g) * (1.0 - lam_init)
    out_a = oa.reshape(b, t, W_A) * jax.nn.silu(ga.astype(f32))

    nc = t // L
    vbn = layer_norm(vb, gmlp_ln_g, gmlp_ln_b)
    causal = jnp.tril(jnp.ones((L, L), dtype=bool))
    ws = jnp.where(causal, gmlp_ws[:, :L, :L], 0.0)
    mixed = (jnp.einsum('gts,bcsgd->bctgd', ws, vbn.reshape(b, nc, L, GB, CG))
             + jnp.swapaxes(gmlp_bs[:, :L], 0, 1)[:, :, None])
    out_b = ub.astype(f32) * mixed.reshape(b, t, W_B) * jax.nn.silu(gb.astype(f32))

    log_g = jnp.log(1.0 - 2.0 ** (-5.0 - jnp.arange(HC, dtype=f32)))
    qc = qc.reshape(b, t, HC, DKC)
    kc = kc.reshape(b, t, HC, DKC) * (DKC ** -0.5)
    vc = vc.reshape(b, t, HC, DVC)
    s0 = jnp.zeros((b, HC, DKC, DVC), f32) if ret_s0 is None else ret_s0.astype(f32)
    oc, ret_s = scan_chunks(lambda q_, k_, v_, s_: retention_chunk(q_, k_, v_, s_, log_g), (qc, kc, vc), s0, L)
    oc = layer_norm(oc, ret_gn_g.reshape(HC, DVC), ret_gn_b.reshape(HC, DVC))
    out_c = oc.reshape(b, t, W_C) * jax.nn.silu(gc.astype(f32))

    buf = jnp.zeros((b, CONV_W - 1, CONV_DIM), xbc.dtype) if conv_buf is None else conv_buf.astype(xbc.dtype)
    xpad = jnp.concatenate([buf, xbc], axis=1)
    conv = conv_b + sum(xpad[:, j:j + t] * conv_w[j] for j in range(CONV_W))
    conv = jax.nn.silu(conv.astype(f32))
    conv_new = xpad[:, t:]
    xd, bd, cd = jnp.split(conv, (W_D, W_D + NG * NS), axis=-1)
    xd = xd.reshape(b, t, HD, PD)
    bd = jnp.repeat(bd.reshape(b, t, NG, NS), HD // NG, axis=2)
    cd = jnp.repeat(cd.reshape(b, t, NG, NS), HD // NG, axis=2)
    dt = jax.nn.softplus(dtd.astype(f32) + dt_bias.astype(f32))
    a = -jnp.exp(a_log.astype(f32))
    h0 = jnp.zeros((b, HD, PD, NS), f32) if ssm_h0 is None else ssm_h0.astype(f32)
    yd, ssm_h = scan_chunks(lambda x_, b_, c_, dt_, h_: ssd_chunk(x_, b_, c_, dt_, a, h_), (xd, bd, cd, dt), h0, L)
    yd = yd + d_skip.astype(f32)[:, None] * xd
    out_d = rms_norm(yd.reshape(b, t, W_D) * jax.nn.silu(zd.astype(f32)), ssd_norm_g)

    mix = jnp.concatenate([out_a, out_b, out_c, out_d], axis=-1).astype(x.dtype)
    h = jnp.einsum('btc,cd->btd', mix, w_out)
    y = layer_norm(ALPHA * x.astype(f32) + h.astype(f32), ln_g, ln_b).astype(x.dtype)
    return y, ka, va, ret_s, conv_new, ssm_h, vbn


def setup_inputs(seed: int = 0) -> dict:
    key = jax.random.key(seed)
    ks = jax.random.split(key, 32)
    f32 = jnp.float32
    n_pages = PAST_LEN // PAGE_SIZE
    n_used = DEC_BATCH * n_pages
    n_pool = (5 * n_used + 3) // 4

    def nrm(k, shape, s=1.0):
        return jax.random.normal(k, shape, f32) * s

    page_table = jax.random.permutation(ks[4], n_pool)[:n_used].reshape(DEC_BATCH, n_pages).astype(jnp.int32)
    dt0 = jnp.exp(jax.random.uniform(ks[25], (DEPTH, HD), f32, math.log(1e-3), math.log(1e-1)))
    dt_bias = dt0 + jnp.log(-jnp.expm1(-dt0))
    a_log = jnp.log(jax.random.uniform(ks[26], (DEPTH, HD), f32, 1.0, 16.0))
    return {
        'x_prompt': nrm(ks[0], (BATCH, SEQ, D_MODEL)),
        'x_sample': nrm(ks[1], (DEC_BATCH, DEC_SEQ, D_MODEL)),
        'cache_k': nrm(ks[2], (DEPTH, n_pool, PAGE_SIZE, HA, 2 * DA)),
        'cache_v': nrm(ks[3], (DEPTH, n_pool, PAGE_SIZE, HA, 2 * DA)),
        'page_table': page_table,
        'state_ret': nrm(ks[5], (DEPTH, DEC_BATCH, HC, DKC, DVC)),
        'state_ssm': nrm(ks[6], (DEPTH, DEC_BATCH, HD, PD, NS), 0.5),
        'state_conv': nrm(ks[7], (DEPTH, DEC_BATCH, CONV_W - 1, CONV_DIM)),
        'w_in': nrm(ks[8], (DEPTH, D_MODEL, IN_COLS), D_MODEL ** -0.5),
        'w_out': nrm(ks[9], (DEPTH, D_MIX, D_MODEL), BETA * D_MIX ** -0.5),
        'ln_g': 1.0 + nrm(ks[10], (DEPTH, D_MODEL), 0.02),
        'ln_b': nrm(ks[11], (DEPTH, D_MODEL), 0.02),
        'lam_q1': nrm(ks[12], (DEPTH, DA), 0.1),
        'lam_k1': nrm(ks[13], (DEPTH, DA), 0.1),
        'lam_q2': nrm(ks[14], (DEPTH, DA), 0.1),
        'lam_k2': nrm(ks[15], (DEPTH, DA), 0.1),
        'subln_g': 1.0 + nrm(ks[16], (DEPTH, 2 * DA), 0.02),
        'gmlp_ln_g': 1.0 + nrm(ks[17], (DEPTH, W_B), 0.02),
        'gmlp_ln_b': nrm(ks[18], (DEPTH, W_B), 0.02),
        'gmlp_ws': nrm(ks[19], (DEPTH, GB, CHUNK, CHUNK), CHUNK ** -0.5),
        'gmlp_bs': 1.0 + nrm(ks[20], (DEPTH, GB, CHUNK), 0.1),
        'ret_gn_g': 1.0 + nrm(ks[21], (DEPTH, W_C), 0.02),
        'ret_gn_b': nrm(ks[22], (DEPTH, W_C), 0.02),
        'conv_w': nrm(ks[23], (DEPTH, CONV_W, CONV_DIM), CONV_W ** -0.5),
        'conv_b': nrm(ks[24], (DEPTH, CONV_DIM), 0.02),
        'dt_bias': dt_bias,
        'a_log': a_log,
        'd_skip': 1.0 + nrm(ks[27], (DEPTH, HD), 0.1),
        'ssd_norm_g': 1.0 + nrm(ks[28], (DEPTH, W_D), 0.02),
    }


def reference(x_prompt, x_sample, cache_k, cache_v, page_table, state_ret, state_ssm, state_conv,
              w_in, w_out, ln_g, ln_b, lam_q1, lam_k1, lam_q2, lam_k2, subln_g, gmlp_ln_g, gmlp_ln_b,
              gmlp_ws, gmlp_bs, ret_gn_g, ret_gn_b, conv_w, conv_b, dt_bias, a_log, d_skip, ssd_norm_g):
    dec_b, n_pages = page_table.shape
    past_len = n_pages * PAGE_SIZE
    yp, ys = x_prompt, x_sample
    kp_l, vp_l, ks_l, vs_l = [], [], [], []
    rp_l, rs_l, hp_l, hs_l, cp_l, cs_l, gs_l = [], [], [], [], [], [], []
    for l in range(DEPTH):
        lw = (w_in[l], w_out[l], ln_g[l], ln_b[l], lam_q1[l], lam_k1[l], lam_q2[l], lam_k2[l], subln_g[l],
              gmlp_ln_g[l], gmlp_ln_b[l], gmlp_ws[l], gmlp_bs[l], ret_gn_g[l], ret_gn_b[l], conv_w[l], conv_b[l],
              dt_bias[l], a_log[l], d_skip[l], ssd_norm_g[l])
        yp, kp, vp, rp, cp, hp, _ = trunk_layer(yp, l, *lw, None, None, None, None, 0)
        k_past = cache_k[l, page_table].reshape(dec_b, past_len, HA, 2 * DA)
        v_past = cache_v[l, page_table].reshape(dec_b, past_len, HA, 2 * DA)
        ys, ksn, vsn, rs, cs, hs, gs = trunk_layer(ys, l, *lw, (k_past, v_past), state_ret[l], state_conv[l],
                                                   state_ssm[l], past_len)
        kp_l.append(kp); vp_l.append(vp); ks_l.append(ksn); vs_l.append(vsn)
        rp_l.append(rp); rs_l.append(rs); hp_l.append(hp); hs_l.append(hs)
        cp_l.append(cp); cs_l.append(cs); gs_l.append(gs)
    return (yp, ys,
            jnp.stack(kp_l), jnp.stack(vp_l), jnp.stack(ks_l), jnp.stack(vs_l),
            jnp.stack(rp_l).astype(state_ret.dtype), jnp.stack(rs_l).astype(state_ret.dtype),
            jnp.stack(hp_l).astype(state_ssm.dtype), jnp.stack(hs_l).astype(state_ssm.dtype),
            jnp.stack(cp_l), jnp.stack(cs_l), jnp.stack(gs_l))
```

```python
import functools
import math

import jax
import jax.numpy as jnp
from jax import lax
from jax.experimental import pallas as pl
from jax.experimental.pallas import tpu as pltpu

F32 = jnp.float32
BF16 = jnp.bfloat16

LANES = 128
TILE_ROWS = 128
HA = 4
DA = 64
GB = 4
HC = 4
DKC = 64
DVC = 128
HD = 8
PD = 64
NS = 128
NG = 2
CONV_W = 4
W_GRP = 512
CONV_DIM = W_GRP + 2 * NG * NS
LN_EPS = 1e-5
NEG_INF = -1e30
PAGE = 128
VMEM_LIMIT = 56 * 1024 * 1024

R_GA, R_UB, R_VB, R_GB = 0, 512, 1024, 1536
R_QC, R_KC, R_VC, R_GC = 2048, 2304, 2560, 3072
R_ZD, R_XBC, R_DT = 3584, 4096, 5120
REST_COLS = 5248
QKV_COLS = 3 * W_GRP


def _dot(a, b):
    return jnp.dot(a, b, preferred_element_type=F32)


def _dot_nt(a, b):
    return lax.dot_general(a, b, (((1,), (1,)), ((), ())), preferred_element_type=F32)


def _dot_tn(a, b):
    return lax.dot_general(a, b, (((0,), (0,)), ((), ())), preferred_element_type=F32)


def _silu(x):
    return x * (1.0 / (1.0 + jnp.exp(-x)))


def _inproj_kernel(x_ref, w_ref, q_ref, k_ref, v_ref, r_ref):
    xb = x_ref[...].astype(BF16)
    q_ref[...] = _dot(xb, w_ref[:, 0:W_GRP])
    k_ref[...] = _dot(xb, w_ref[:, W_GRP:2 * W_GRP])
    v_ref[...] = _dot(xb, w_ref[:, 2 * W_GRP:3 * W_GRP])
    r_ref[...] = _dot(xb, w_ref[:, QKV_COLS:])


def _inproj(x2, w_pad, tm):
    t, d = x2.shape
    ncols = w_pad.shape[1]
    return pl.pallas_call(
        _inproj_kernel,
        grid=(t // tm,),
        in_specs=[pl.BlockSpec((tm, d), lambda i: (i, 0)),
                  pl.BlockSpec((d, ncols), lambda i: (0, 0), pipeline_mode=pl.Buffered(1))],
        out_specs=[pl.BlockSpec((tm, W_GRP), lambda i: (i, 0)),
                   pl.BlockSpec((tm, W_GRP), lambda i: (i, 0)),
                   pl.BlockSpec((tm, W_GRP), lambda i: (i, 0)),
                   pl.BlockSpec((tm, REST_COLS), lambda i: (i, 0))],
        out_shape=[jax.ShapeDtypeStruct((t, W_GRP), F32),
                   jax.ShapeDtypeStruct((t, W_GRP), F32),
                   jax.ShapeDtypeStruct((t, W_GRP), F32),
                   jax.ShapeDtypeStruct((t, REST_COLS), F32)],
        compiler_params=pltpu.CompilerParams(dimension_semantics=("arbitrary",),
                                             vmem_limit_bytes=VMEM_LIMIT),
        name="inproj",
    )(x2, w_pad)


def _split_maps(q):
    lane = lax.broadcasted_iota(jnp.int32, q.shape, 1)
    q1 = jnp.where(lane < DA, q, 0.0)
    q2 = jnp.where(lane >= DA, q, 0.0)
    return jnp.concatenate([q1, q2], axis=0).astype(BF16)


def _lambda_value(lam_ref, lam_init):
    lv = lam_ref[...]
    a = jnp.sum(lv[0:1, :] * lv[1:2, :], axis=1, keepdims=True)
    b = jnp.sum(lv[2:3, :] * lv[3:4, :], axis=1, keepdims=True)
    return jnp.exp(a) - jnp.exp(b) + lam_init


def _online_update(carry, s, vb):
    m, l, acc = carry
    m_new = jnp.maximum(m, jnp.max(s, axis=1, keepdims=True))
    alpha = jnp.exp(m - m_new)
    p = jnp.exp(s - m_new)
    l = alpha * l + jnp.sum(p, axis=1, keepdims=True)
    acc = alpha * acc + _dot(p.astype(BF16), vb)
    return m_new, l, acc


def _finish_head(carry, n, lam, lam_init, sg, ga):
    _, l, acc = carry
    o = acc / l
    out = o[:n] - lam * o[n:]
    out = out * lax.rsqrt(jnp.mean(out * out, axis=1, keepdims=True) + LN_EPS) * sg * (1.0 - lam_init)
    return out * _silu(ga)


def _attn_prompt_kernel(lam_ref, q_ref, k_ref, v_ref, ga_ref, sg_ref, o_ref, *, tq, lam_init):
    h = pl.program_id(1)
    i = pl.program_id(2)
    slope = jnp.where(h == 0, 2.0 ** -2, jnp.where(h == 1, 2.0 ** -4,
                      jnp.where(h == 2, 2.0 ** -6, 2.0 ** -8))).astype(F32)
    q2 = _split_maps(q_ref[...] * (DA ** -0.5))
    col = lax.broadcasted_iota(jnp.int32, (1, tq), 1)

    def block(j, carry, masked):
        start = pl.multiple_of(j * tq, tq)
        kb = k_ref[pl.ds(start, tq), :].astype(BF16)
        vb = v_ref[pl.ds(start, tq), :].astype(BF16)
        s = _dot_nt(q2, kb)
        s = s + slope * (col + (j - i) * tq).astype(F32)
        if masked:
            r = lax.broadcasted_iota(jnp.int32, (2 * tq, tq), 0)
            r = jnp.where(r >= tq, r - tq, r)
            c = lax.broadcasted_iota(jnp.int32, (2 * tq, tq), 1)
            s = jnp.where(c <= r, s, NEG_INF)
        return _online_update(carry, s, vb)

    init = (jnp.full((2 * tq, 1), NEG_INF, F32), jnp.zeros((2 * tq, 1), F32),
            jnp.zeros((2 * tq, LANES), F32))
    carry = lax.fori_loop(0, i, lambda j, c: block(j, c, False), init)
    carry = block(i, carry, True)
    lam = _lambda_value(lam_ref, lam_init)
    o_ref[...] = _finish_head(carry, tq, lam, lam_init, sg_ref[...], ga_ref[...])


def _attn_prompt(q, k, v, rest, lamv, subln, *, batch, seq, tq, lam_init):
    t = batch * seq
    nq = seq // tq
    kern = functools.partial(_attn_prompt_kernel, tq=tq, lam_init=lam_init)
    return pl.pallas_call(
        kern,
        grid=(batch, HA, nq),
        in_specs=[pl.BlockSpec((4, DA), lambda b, h, i: (0, 0)),
                  pl.BlockSpec((tq, LANES), lambda b, h, i: (b * nq + i, h)),
                  pl.BlockSpec((seq, LANES), lambda b, h, i: (b, h)),
                  pl.BlockSpec((seq, LANES), lambda b, h, i: (b, h)),
                  pl.BlockSpec((tq, LANES), lambda b, h, i: (b * nq + i, h)),
                  pl.BlockSpec((1, LANES), lambda b, h, i: (0, 0))],
        out_specs=pl.BlockSpec((tq, LANES), lambda b, h, i: (b * nq + i, h)),
        out_shape=jax.ShapeDtypeStruct((t, W_GRP), F32),
        compiler_params=pltpu.CompilerParams(
            dimension_semantics=("arbitrary", "arbitrary", "arbitrary"), vmem_limit_bytes=VMEM_LIMIT),
        name="attn_prompt",
    )(lamv, q, k, v, rest, subln)


def _attn_decode_kernel(pt_ref, lam_ref, q_ref, kn_ref, vn_ref, ga_ref, sg_ref, *refs,
                        npg, past_len, lam_init):
    del pt_ref
    k_refs = refs[:npg]
    v_refs = refs[npg:2 * npg]
    o_ref = refs[2 * npg]
    m_s, l_s, acc_s = refs[2 * npg + 1:]
    j = pl.program_id(1)
    nj = pl.num_programs(1)
    nq = q_ref.shape[0]
    blk = npg * PAGE

    @pl.when(j == 0)
    def _():
        m_s[...] = jnp.full(m_s.shape, NEG_INF, F32)
        l_s[...] = jnp.zeros(l_s.shape, F32)
        acc_s[...] = jnp.zeros(acc_s.shape, F32)

    col = lax.broadcasted_iota(jnp.int32, (1, blk), 1)
    for h in range(HA):
        hs = slice(h * LANES, (h + 1) * LANES)
        slope = 2.0 ** (-8.0 * (h + 1) / HA)
        q2 = _split_maps(q_ref[:, hs] * (DA ** -0.5))
        kc = jnp.concatenate([r[:, hs] for r in k_refs], axis=0).astype(BF16)
        vc = jnp.concatenate([r[:, hs] for r in v_refs], axis=0).astype(BF16)
        s = _dot_nt(q2, kc)
        s = s + slope * (col + (j * blk - past_len)).astype(F32)
        m, l, acc = _online_update((m_s[h], l_s[h], acc_s[h]), s, vc)
        m_s[h] = m
        l_s[h] = l
        acc_s[h] = acc

    @pl.when(j == nj - 1)
    def _():
        lam = _lambda_value(lam_ref, lam_init)
        r = lax.broadcasted_iota(jnp.int32, (2 * nq, LANES), 0)
        r = jnp.where(r >= nq, r - nq, r)
        c = lax.broadcasted_iota(jnp.int32, (2 * nq, LANES), 1)
        zpad = jnp.zeros((LANES - nq, LANES), F32)
        for h in range(HA):
            hs = slice(h * LANES, (h + 1) * LANES)
            slope = 2.0 ** (-8.0 * (h + 1) / HA)
            q2 = _split_maps(q_ref[:, hs] * (DA ** -0.5))
            kn = jnp.concatenate([kn_ref[:, hs], zpad], axis=0).astype(BF16)
            vn = jnp.concatenate([vn_ref[:, hs], zpad], axis=0).astype(BF16)
            s = _dot_nt(q2, kn) + slope * c.astype(F32)
            s = jnp.where(c <= r, s, NEG_INF)
            carry = _online_update((m_s[h], l_s[h], acc_s[h]), s, vn)
            o_ref[:, hs] = _finish_head(carry, nq, lam, lam_init, sg_ref[...], ga_ref[:, hs])


def _attn_decode(pt_flat, q, kn, vn, rest, lamv, subln, cache_k, cache_v, *, layer, dec_b, dec_t,
                 n_pages, npg, lam_init):
    past_len = n_pages * PAGE
    kern = functools.partial(_attn_decode_kernel, npg=npg, past_len=past_len, lam_init=lam_init)

    def page_spec(i):
        return pl.BlockSpec((None, None, PAGE, W_GRP),
                            lambda b, j, pt: (layer, pt[b * n_pages + j * npg + i], 0, 0))

    row_spec = pl.BlockSpec((dec_t, W_GRP), lambda b, j, pt: (b, 0))
    grid_spec = pltpu.PrefetchScalarGridSpec(
        num_scalar_prefetch=1,
        grid=(dec_b, n_pages // npg),
        in_specs=[pl.BlockSpec((4, DA), lambda b, j, pt: (0, 0)),
                  row_spec, row_spec, row_spec, row_spec,
                  pl.BlockSpec((1, LANES), lambda b, j, pt: (0, 0))]
                 + [page_spec(i) for i in range(npg)] + [page_spec(i) for i in range(npg)],
        out_specs=row_spec,
        scratch_shapes=[pltpu.VMEM((HA, 2 * dec_t, 1), F32),
                        pltpu.VMEM((HA, 2 * dec_t, 1), F32),
                        pltpu.VMEM((HA, 2 * dec_t, LANES), F32)],
    )
    return pl.pallas_call(
        kern,
        grid_spec=grid_spec,
        out_shape=jax.ShapeDtypeStruct((dec_b * dec_t, W_GRP), F32),
        compiler_params=pltpu.CompilerParams(dimension_semantics=("arbitrary", "arbitrary"),
                                             vmem_limit_bytes=VMEM_LIMIT),
        name="attn_decode",
    )(pt_flat, lamv, q, kn, vn, rest, subln, *([cache_k] * npg), *([cache_v] * npg))


def _mixer_kernel(rest_ref, ret0_ref, ssm0_ref, conv0_ref, glg_ref, glb_ref, ws_ref, bs_ref,
                  rgg_ref, rgb_ref, cw_ref, cb_ref, dtb_ref, alog_ref, dskip_ref, sng_ref,
                  *refs, seg_len, n_seg, emit_vbn):
    if emit_vbn:
        out_ref, ret_ref, ssm_ref, convn_ref, vbn_ref, xpad_s, dl_s = refs
    else:
        out_ref, ret_ref, ssm_ref, convn_ref, xpad_s, dl_s = refs
        vbn_ref = None
    L, G = seg_len, n_seg
    R = TILE_ROWS
    lg = int(math.log2(L))
    c = pl.program_id(1)

    @pl.when(c == 0)
    def _():
        ret_ref[...] = ret0_ref[...]
        ssm_ref[...] = ssm0_ref[...]
        xpad_s[:, 8 - (CONV_W - 1):8, :] = conv0_ref[...]

    tt = lax.broadcasted_iota(jnp.int32, (R, R), 0)
    ss = lax.broadcasted_iota(jnp.int32, (R, R), 1)
    same = (tt >> lg) == (ss >> lg)
    mask = same & (ss <= tt)
    diff = jnp.maximum(tt - ss, 0).astype(F32)
    trow = lax.broadcasted_iota(jnp.int32, (R, 1), 0)
    tpos = (trow & (L - 1)).astype(F32)
    rseg = trow >> lg

    def over_segments(body, init):
        if G == 1:
            return body(0, None, init)
        return lax.fori_loop(0, G, lambda j, cr: body(j, rseg == j, cr), init)

    def rows_of(rm, x):
        return x if rm is None else jnp.where(rm, x, 0.0)

    vb = rest_ref[:, R_VB:R_VB + W_GRP]
    mu = jnp.mean(vb, axis=1, keepdims=True)
    vc_ = vb - mu
    var = jnp.mean(vc_ * vc_, axis=1, keepdims=True)
    vbn = vc_ * lax.rsqrt(var + LN_EPS) * glg_ref[...] + glb_ref[...]
    if emit_vbn:
        vbn_ref[...] = vbn
    for g in range(GB):
        gs = slice(g * LANES, (g + 1) * LANES)
        w = jnp.where(mask, ws_ref[g], 0.0).astype(BF16)
        mixed = _dot(w, vbn[:, gs].astype(BF16)) + bs_ref[:, g:g + 1]
        ub = rest_ref[:, R_UB + g * LANES:R_UB + (g + 1) * LANES]
        gb = rest_ref[:, R_GB + g * LANES:R_GB + (g + 1) * LANES]
        out_ref[:, gs] = ub * mixed * _silu(gb)

    for h in range(HC):
        log_g = math.log(1.0 - 2.0 ** (-5.0 - h))
        q = rest_ref[:, R_QC + h * DKC:R_QC + (h + 1) * DKC].astype(BF16)
        k = rest_ref[:, R_KC + h * DKC:R_KC + (h + 1) * DKC] * (DKC ** -0.5)
        v = rest_ref[:, R_VC + h * DVC:R_VC + (h + 1) * DVC]
        decay = jnp.where(mask, jnp.exp(log_g * diff), 0.0)
        scores = _dot_nt(q, k.astype(BF16)) * decay
        o = _dot(scores.astype(BF16), v.astype(BF16))
        row_scale = jnp.exp(log_g * (tpos + 1.0))
        vw = (v * jnp.exp(log_g * (L - 1.0 - tpos))).astype(BF16)
        g_len = math.exp(log_g * L)

        def ret_body(j, rm, o_state, h=h, q=q, k=k, vw=vw, g_len=g_len):
            s0 = ret_ref[j, h]
            o_state = o_state + rows_of(rm, _dot(q, s0.astype(BF16)))
            ret_ref[j, h] = g_len * s0 + _dot_tn(rows_of(rm, k).astype(BF16), vw)
            return o_state

        o = o + over_segments(ret_body, jnp.zeros((R, DVC), F32)) * row_scale
        mu = jnp.mean(o, axis=1, keepdims=True)
        oc = o - mu
        var = jnp.mean(oc * oc, axis=1, keepdims=True)
        hs = slice(h * DVC, (h + 1) * DVC)
        oc = oc * lax.rsqrt(var + LN_EPS) * rgg_ref[:, hs] + rgb_ref[:, hs]
        gc = rest_ref[:, R_GC + h * DVC:R_GC + (h + 1) * DVC]
        out_ref[:, W_GRP + h * DVC:W_GRP + (h + 1) * DVC] = oc * _silu(gc)

    x_cur = rest_ref[:, R_XBC:R_XBC + CONV_DIM]
    xpad_s[:, 8:8 + L, :] = x_cur.reshape(G, L, CONV_DIM)
    conv = cb_ref[...] + cw_ref[CONV_W - 1:CONV_W, :] * x_cur
    for kk in range(1, CONV_W):
        shifted = xpad_s[:, 8 - kk:8 - kk + L, :].reshape(R, CONV_DIM)
        conv = conv + cw_ref[CONV_W - 1 - kk:CONV_W - kk, :] * shifted
    convn_ref[...] = xpad_s[:, L + 8 - (CONV_W - 1):L + 8, :]
    xpad_s[:, 0:8, :] = xpad_s[:, L:L + 8, :]
    conv = _silu(conv)
    xd = conv[:, :W_GRP]
    bd = conv[:, W_GRP:W_GRP + NG * NS]
    cd = conv[:, W_GRP + NG * NS:]

    zt = rest_ref[:, R_DT:R_DT + LANES] + dtb_ref[...]
    dt = jnp.maximum(zt, 0.0) + jnp.log1p(jnp.exp(-jnp.abs(zt)))
    dta = dt * (-jnp.exp(alog_ref[...]))
    cum = jnp.dot(mask.astype(F32), dta, precision=lax.Precision.HIGHEST, preferred_element_type=F32)
    cum_last = jnp.dot(same.astype(F32), dta, precision=lax.Precision.HIGHEST,
                       preferred_element_type=F32)
    cum_t = cum.T
    dt_t = dt.T
    ecum = jnp.exp(cum)
    wlast = jnp.exp(cum_last - cum) * dt
    dl_s[...] = jnp.exp(cum_last)

    yd_parts = []
    for grp in range(NG):
        bg = bd[:, grp * NS:(grp + 1) * NS].astype(BF16)
        cg = cd[:, grp * NS:(grp + 1) * NS].astype(BF16)
        cb = _dot_nt(cg, bg)
        for hh in range(HD // NG):
            h = grp * (HD // NG) + hh
            seg = cum[:, h:h + 1] - cum_t[h:h + 1, :]
            lmat = jnp.where(mask, jnp.exp(jnp.where(mask, seg, 0.0)), 0.0)
            scores = cb * lmat * dt_t[h:h + 1, :]
            x_h = xd[:, h * PD:(h + 1) * PD]
            y = _dot(scores.astype(BF16), x_h.astype(BF16))
            xw = x_h * wlast[:, h:h + 1]

            def ssd_body(j, rm, y_state, h=h, cg=cg, bg=bg, xw=xw):
                h0 = ssm_ref[j, h]
                y_state = y_state + rows_of(rm, _dot_nt(cg, h0.astype(BF16)))
                dec = dl_s[pl.ds(j * L, 1), :][:, h:h + 1]
                ssm_ref[j, h] = dec * h0 + _dot_tn(rows_of(rm, xw).astype(BF16), bg)
                return y_state

            y = y + over_segments(ssd_body, jnp.zeros((R, PD), F32)) * ecum[:, h:h + 1]
            yd_parts.append(y + dskip_ref[:, h * PD:(h + 1) * PD] * x_h)
    yd = jnp.concatenate(yd_parts, axis=1)
    z = yd * _silu(rest_ref[:, R_ZD:R_ZD + W_GRP])
    out_ref[:, 2 * W_GRP:3 * W_GRP] = (
        z * lax.rsqrt(jnp.mean(z * z, axis=1, keepdims=True) + LN_EPS) * sng_ref[...])


def _mixers(rest, ret0, ssm0, conv0, p, *, seg_len, n_chunks, emit_vbn):
    t = rest.shape[0]
    n_seq = ret0.shape[0]
    G = TILE_ROWS // seg_len
    n_grp = n_seq // G
    kern = functools.partial(_mixer_kernel, seg_len=seg_len, n_seg=G, emit_vbn=emit_vbn)

    def full(shape):
        nd = len(shape)
        return pl.BlockSpec(shape, lambda g, c, _nd=nd: (0,) * _nd)

    in_specs = [pl.BlockSpec((TILE_ROWS, REST_COLS), lambda g, c: (g * n_chunks + c, 0)),
                pl.BlockSpec((G, HC, DKC, DVC), lambda g, c: (g, 0, 0, 0)),
                pl.BlockSpec((G, HD, PD, NS), lambda g, c: (g, 0, 0, 0)),
                pl.BlockSpec((G, CONV_W - 1, CONV_DIM), lambda g, c: (g, 0, 0)),
                full((1, W_GRP)), full((1, W_GRP)), full((GB, TILE_ROWS, TILE_ROWS)),
                full((TILE_ROWS, GB)), full((1, W_GRP)), full((1, W_GRP)),
                full((CONV_W, CONV_DIM)), full((1, CONV_DIM)), full((1, LANES)), full((1, LANES)),
                full((1, W_GRP)), full((1, W_GRP))]
    out_specs = [pl.BlockSpec((TILE_ROWS, 3 * W_GRP), lambda g, c: (g * n_chunks + c, 0)),
                 pl.BlockSpec((G, HC, DKC, DVC), lambda g, c: (g, 0, 0, 0)),
                 pl.BlockSpec((G, HD, PD, NS), lambda g, c: (g, 0, 0, 0)),
                 pl.BlockSpec((G, CONV_W - 1, CONV_DIM), lambda g, c: (g, 0, 0))]
    out_shape = [jax.ShapeDtypeStruct((t, 3 * W_GRP), F32),
                 jax.ShapeDtypeStruct((n_seq, HC, DKC, DVC), F32),
                 jax.ShapeDtypeStruct((n_seq, HD, PD, NS), F32),
                 jax.ShapeDtypeStruct((n_seq, CONV_W - 1, CONV_DIM), F32)]
    if emit_vbn:
        out_specs.append(pl.BlockSpec((TILE_ROWS, W_GRP), lambda g, c: (g * n_chunks + c, 0)))
        out_shape.append(jax.ShapeDtypeStruct((t, W_GRP), F32))
    return pl.pallas_call(
        kern,
        grid=(n_grp, n_chunks),
        in_specs=in_specs,
        out_specs=out_specs,
        out_shape=out_shape,
        scratch_shapes=[pltpu.VMEM((G, seg_len + 8, CONV_DIM), F32),
                        pltpu.VMEM((TILE_ROWS, LANES), F32)],
        compiler_params=pltpu.CompilerParams(dimension_semantics=("arbitrary", "arbitrary"),
                                             vmem_limit_bytes=VMEM_LIMIT),
        name="mixers",
    )(rest, ret0, ssm0, conv0, p["glg"], p["glb"], p["ws"], p["bs"], p["rgg"], p["rgb"],
      p["cw"], p["cb"], p["dtb"], p["alog"], p["dskip"], p["sng"])


def _outproj_kernel(a_ref, m_ref, x_ref, w_ref, g_ref, b_ref, y_ref, *, alpha):
    hproj = _dot(a_ref[...].astype(BF16), w_ref[0:W_GRP, :])
    hproj = hproj + _dot(m_ref[...].astype(BF16), w_ref[W_GRP:, :])
    z = alpha * x_ref[...] + hproj
    mu = jnp.mean(z, axis=1, keepdims=True)
    zc = z - mu
    var = jnp.mean(zc * zc, axis=1, keepdims=True)
    y_ref[...] = zc * lax.rsqrt(var + LN_EPS) * g_ref[...] + b_ref[...]


def _outproj(out_a, out_bcd, x2, w_out, ln_g, ln_b, tm, alpha):
    t, d = x2.shape
    kern = functools.partial(_outproj_kernel, alpha=alpha)
    return pl.pallas_call(
        kern,
        grid=(t // tm,),
        in_specs=[pl.BlockSpec((tm, W_GRP), lambda i: (i, 0)),
                  pl.BlockSpec((tm, 3 * W_GRP), lambda i: (i, 0)),
                  pl.BlockSpec((tm, d), lambda i: (i, 0)),
                  pl.BlockSpec((4 * W_GRP, d), lambda i: (0, 0)),
                  pl.BlockSpec((1, d), lambda i: (0, 0)),
                  pl.BlockSpec((1, d), lambda i: (0, 0))],
        out_specs=pl.BlockSpec((tm, d), lambda i: (i, 0)),
        out_shape=jax.ShapeDtypeStruct((t, d), F32),
        compiler_params=pltpu.CompilerParams(dimension_semantics=("arbitrary",),
                                             vmem_limit_bytes=VMEM_LIMIT),
        name="outproj",
    )(out_a, out_bcd, x2, w_out, ln_g, ln_b)


def _layer_params(l, seg_len, gmlp_ln_g, gmlp_ln_b, gmlp_ws, gmlp_bs, ret_gn_g, ret_gn_b, conv_w,
                  conv_b, dt_bias, a_log, d_skip, ssd_norm_g):
    G = TILE_ROWS // seg_len
    pad8 = lambda v: jnp.pad(v.astype(F32), (0, LANES - HD)).reshape(1, LANES)
    return {
        "glg": gmlp_ln_g[l].reshape(1, W_GRP), "glb": gmlp_ln_b[l].reshape(1, W_GRP),
        "ws": jnp.tile(gmlp_ws[l][:, :seg_len, :seg_len], (1, G, G)),
        "bs": jnp.tile(gmlp_bs[l][:, :seg_len].T, (G, 1)),
        "rgg": ret_gn_g[l].reshape(1, W_GRP), "rgb": ret_gn_b[l].reshape(1, W_GRP),
        "cw": conv_w[l], "cb": conv_b[l].reshape(1, CONV_DIM),
        "dtb": pad8(dt_bias[l]), "alog": pad8(a_log[l]),
        "dskip": jnp.repeat(d_skip[l].astype(F32), PD).reshape(1, W_GRP),
        "sng": ssd_norm_g[l].reshape(1, W_GRP),
    }


def kernel(x_prompt, x_sample, cache_k, cache_v, page_table, state_ret, state_ssm, state_conv, w_in, w_out, ln_g, ln_b, lam_q1, lam_k1, lam_q2, lam_k2, subln_g, gmlp_ln_g, gmlp_ln_b, gmlp_ws, gmlp_bs, ret_gn_g, ret_gn_b, conv_w, conv_b, dt_bias, a_log, d_skip, ssd_norm_g):
    depth = w_in.shape[0]
    batch, seq, d_model = x_prompt.shape
    dec_b, dec_t, _ = x_sample.shape
    n_pages = page_table.shape[1]
    in_cols = w_in.shape[2]
    alpha = (2.0 * depth) ** 0.25
    assert in_cols == QKV_COLS + R_DT + HD and seq % TILE_ROWS == 0 and TILE_ROWS % dec_t == 0
    assert (dec_b * dec_t) % TILE_ROWS == 0

    w_in_p = jnp.pad(w_in.astype(BF16), ((0, 0), (0, 0), (0, QKV_COLS + REST_COLS - in_cols)))
    w_out_b = w_out.astype(BF16)
    ck = cache_k.reshape(cache_k.shape[0], cache_k.shape[1], PAGE, W_GRP)
    cv = cache_v.reshape(cache_v.shape[0], cache_v.shape[1], PAGE, W_GRP)
    pt_flat = page_table.reshape(-1).astype(jnp.int32)
    zeros_ret = jnp.zeros((batch, HC, DKC, DVC), F32)
    zeros_ssm = jnp.zeros((batch, HD, PD, NS), F32)
    zeros_conv = jnp.zeros((batch, CONV_W - 1, CONV_DIM), F32)
    mix_args = (gmlp_ln_g, gmlp_ln_b, gmlp_ws, gmlp_bs, ret_gn_g, ret_gn_b, conv_w, conv_b, dt_bias,
                a_log, d_skip, ssd_norm_g)

    tp = batch * seq
    tsm = dec_b * dec_t
    tq = min(256, seq)
    npg = 8 if n_pages % 8 == 0 else 1
    yp = x_prompt.reshape(tp, d_model)
    ys = x_sample.reshape(tsm, d_model)
    outs = [[] for _ in range(11)]
    for l in range(depth):
        lam_init = 0.8 - 0.6 * math.exp(-0.3 * l)
        lamv = jnp.stack([lam_q1[l], lam_k1[l], lam_q2[l], lam_k2[l]]).astype(F32)
        subln = subln_g[l].reshape(1, LANES)
        lng = ln_g[l].reshape(1, d_model)
        lnb = ln_b[l].reshape(1, d_model)

        pp = _layer_params(l, TILE_ROWS, *mix_args)
        q, k, v, rest = _inproj(yp, w_in_p[l], 256)
        oa = _attn_prompt(q, k, v, rest, lamv, subln, batch=batch, seq=seq, tq=tq, lam_init=lam_init)
        obcd, rp, hp, cp = _mixers(rest, zeros_ret, zeros_ssm, zeros_conv, pp, seg_len=TILE_ROWS,
                                   n_chunks=seq // TILE_ROWS, emit_vbn=False)
        yp = _outproj(oa, obcd, yp, w_out_b[l], lng, lnb, 256, alpha)

        ps = _layer_params(l, dec_t, *mix_args)
        qs, ksn, vsn, rest_s = _inproj(ys, w_in_p[l], min(256, tsm))
        oas = _attn_decode(pt_flat, qs, ksn, vsn, rest_s, lamv, subln, ck, cv, layer=l, dec_b=dec_b,
                           dec_t=dec_t, n_pages=n_pages, npg=npg, lam_init=lam_init)
        obcd_s, rs, hs, cs, gs = _mixers(rest_s, state_ret[l], state_ssm[l], state_conv[l], ps,
                                         seg_len=dec_t, n_chunks=1, emit_vbn=True)
        ys = _outproj(oas, obcd_s, ys, w_out_b[l], lng, lnb, min(256, tsm), alpha)

        for lst, val in zip(outs, (k.reshape(batch, seq, HA, 2 * DA), v.reshape(batch, seq, HA, 2 * DA),
                                   ksn.reshape(dec_b, dec_t, HA, 2 * DA),
                                   vsn.reshape(dec_b, dec_t, HA, 2 * DA),
                                   rp, rs, hp, hs, cp, cs, gs.reshape(dec_b, dec_t, W_GRP))):
            lst.append(val)
    stacked = [jnp.stack(o) for o in outs]
    return (yp.reshape(batch, seq, d_model), ys.reshape(dec_b, dec_t, d_model), *stacked)
```

```python
import functools
import math

import jax
import jax.numpy as jnp
from jax import lax
from jax.experimental import pallas as pl
from jax.experimental.pallas import tpu as pltpu

F32 = jnp.float32
BF16 = jnp.bfloat16

LANES = 128
TILE_ROWS = 128
HA = 4
DA = 64
GB = 4
HC = 4
DKC = 64
DVC = 128
HD = 8
PD = 64
NS = 128
NG = 2
CONV_W = 4
W_GRP = 512
CONV_DIM = W_GRP + 2 * NG * NS
LN_EPS = 1e-5
NEG_INF = -1e30
PAGE = 128
VMEM_LIMIT = 56 * 1024 * 1024
LOG2E = math.log2(math.e)
Q_SCALE = DA ** -0.5 * LOG2E

R_GA, R_UB, R_VB, R_GB = 0, 512, 1024, 1536
R_QC, R_KC, R_VC, R_GC = 2048, 2304, 2560, 3072
R_ZD, R_XBC, R_DT = 3584, 4096, 5120
REST_COLS = 5248
QKV_COLS = 3 * W_GRP


def _dot(a, b):
    return jnp.dot(a, b, preferred_element_type=F32)


def _dot_nt(a, b):
    return lax.dot_general(a, b, (((1,), (1,)), ((), ())), preferred_element_type=F32)


def _dot_tn(a, b):
    return lax.dot_general(a, b, (((0,), (0,)), ((), ())), preferred_element_type=F32)


def _silu(x):
    return x * (1.0 / (1.0 + jnp.exp(-x)))


def _alibi_slope2(h):
    return 2.0 ** (-8.0 * (h + 1) / HA) * LOG2E


def _inproj_kernel(x_ref, w_ref, *out_refs, with_bf16_kv):
    if with_bf16_kv:
        q_ref, k_ref, v_ref, kb_ref, vb_ref, r_ref = out_refs
    else:
        q_ref, k_ref, v_ref, r_ref = out_refs
    xb = x_ref[...].astype(BF16)
    q_ref[...] = (_dot(xb, w_ref[:, 0:W_GRP]) * Q_SCALE).astype(q_ref.dtype)
    k = _dot(xb, w_ref[:, W_GRP:2 * W_GRP])
    v = _dot(xb, w_ref[:, 2 * W_GRP:3 * W_GRP])
    for h in range(HA):
        k_ref[:, h, :] = k[:, h * LANES:(h + 1) * LANES]
        v_ref[:, h, :] = v[:, h * LANES:(h + 1) * LANES]
    if with_bf16_kv:
        kb_ref[...] = k.astype(BF16)
        vb_ref[...] = v.astype(BF16)
    r_ref[...] = _dot(xb, w_ref[:, QKV_COLS:])


def _inproj(x2, w_pad, tm, *, with_bf16_kv):
    t, d = x2.shape
    ncols = w_pad.shape[1]
    row = lambda i: (i, 0)
    out_specs = [pl.BlockSpec((tm, W_GRP), row),
                 pl.BlockSpec((tm, HA, LANES), lambda i: (i, 0, 0)),
                 pl.BlockSpec((tm, HA, LANES), lambda i: (i, 0, 0))]
    out_shape = [jax.ShapeDtypeStruct((t, W_GRP), BF16 if with_bf16_kv else F32),
                 jax.ShapeDtypeStruct((t, HA, LANES), F32),
                 jax.ShapeDtypeStruct((t, HA, LANES), F32)]
    if with_bf16_kv:
        out_specs += [pl.BlockSpec((tm, W_GRP), row), pl.BlockSpec((tm, W_GRP), row)]
        out_shape += [jax.ShapeDtypeStruct((t, W_GRP), BF16), jax.ShapeDtypeStruct((t, W_GRP), BF16)]
    out_specs.append(pl.BlockSpec((tm, REST_COLS), row))
    out_shape.append(jax.ShapeDtypeStruct((t, REST_COLS), F32))
    return pl.pallas_call(
        functools.partial(_inproj_kernel, with_bf16_kv=with_bf16_kv),
        grid=(t // tm,),
        in_specs=[pl.BlockSpec((tm, d), row),
                  pl.BlockSpec((d, ncols), lambda i: (0, 0), pipeline_mode=pl.Buffered(1))],
        out_specs=out_specs,
        out_shape=out_shape,
        compiler_params=pltpu.CompilerParams(dimension_semantics=("arbitrary",),
                                             vmem_limit_bytes=VMEM_LIMIT),
        name="inproj",
    )(x2, w_pad)


def _split_maps(q):
    lane = lax.broadcasted_iota(jnp.int32, q.shape, 1)
    zero = jnp.zeros_like(q)
    q1 = jnp.where(lane < DA, q, zero)
    q2 = jnp.where(lane >= DA, q, zero)
    return jnp.concatenate([q1, q2], axis=0).astype(BF16)


def _lambda_value(lam_ref, lam_init):
    lv = lam_ref[...]
    a = jnp.sum(lv[0:1, :] * lv[1:2, :], axis=1, keepdims=True)
    b = jnp.sum(lv[2:3, :] * lv[3:4, :], axis=1, keepdims=True)
    return jnp.exp(a) - jnp.exp(b) + lam_init


def _finish_head(o, n, lam, lam_init, sg, ga):
    out = o[:n] - lam * o[n:]
    out = out * lax.rsqrt(jnp.mean(out * out, axis=1, keepdims=True) + LN_EPS) * sg * (1.0 - lam_init)
    return out * _silu(ga)


def _attn_prompt_kernel(lam_ref, q_ref, k_ref, v_ref, ga_ref, sg_ref, o_ref,
                        q2_s, s0_s, s1_s, p0_s, p1_s, m_s, a_s, l_s, acc_s, *, tq, lam_init):
    h = pl.program_id(1)
    i = pl.program_id(2)
    slope = (jnp.where(h == 0, 2.0 ** -2, jnp.where(h == 1, 2.0 ** -4,
                       jnp.where(h == 2, 2.0 ** -6, 2.0 ** -8))) * LOG2E).astype(F32)
    s_bufs = (s0_s, s1_s)
    p_bufs = (p0_s, p1_s)
    q2_s[...] = _split_maps(q_ref[...])
    m_s[...] = jnp.full(m_s.shape, NEG_INF, F32)
    l_s[...] = jnp.zeros(l_s.shape, F32)
    acc_s[...] = jnp.zeros(acc_s.shape, F32)
    col =lax.broadcasted_iota(jnp.int32, (1, tq), 1)

    def tile(ref, j):
        return ref[pl.ds(pl.multiple_of(j * tq, tq), tq), :]

    def qk(j, slot):
        s_bufs[slot][...] = _dot_nt(q2_s[...], tile(k_ref, j))

    def pv(j, slot):
        acc_s[...] = a_s[...] * acc_s[...] + _dot(p_bufs[slot][...], tile(v_ref, j))

    def softmax(j, slot, masked):
        bias = slope * (col + (j - i) * tq).astype(F32)
        if masked:
            r = lax.broadcasted_iota(jnp.int32, (2 * tq, LANES), 0) & (tq - 1)
            c = lax.broadcasted_iota(jnp.int32, (2 * tq, LANES), 1)
        sb = []
        for cb in range(tq // LANES):
            cols = slice(cb * LANES, (cb + 1) * LANES)
            x = s_bufs[slot][:, cols] + bias[:, cols]
            if masked:
                x = jnp.where(c + cb * LANES <= r, x, NEG_INF)
            sb.append(x)
        m_old = m_s[...]
        m_new = jnp.maximum(m_old, jnp.max(functools.reduce(jnp.maximum, sb), axis=1, keepdims=True))
        alpha = jnp.exp2(m_old - m_new)
        pb = [jnp.exp2(x - m_new) for x in sb]
        l_s[...] = alpha * l_s[...] + functools.reduce(jnp.add, pb)
        for cb in range(tq // LANES):
            p_bufs[slot][:, cb * LANES:(cb + 1) * LANES] = pb[cb].astype(BF16)
        m_s[...] = m_new
        a_s[...] = alpha

    def step(j, cur):
        qk(j + 1, 1 - cur)
        pv(j - 1, 1 - cur)
        softmax(j, cur, False)

    def last(cur):
        pv(i - 1, 1 - cur)
        softmax(i, cur, True)
        pv(i, cur)

    def by_parity(j, fn):
        pl.when((j & 1) == 0)(lambda: fn(0))
        pl.when((j & 1) == 1)(lambda: fn(1))

    qk(0, 0)

    @pl.when(i == 0)
    def _():
        softmax(0, 0, True)
        pv(0, 0)

    @pl.when(i > 0)
    def _():
        qk(1, 1)
        softmax(0, 0, False)

    def body(j, carry):
        by_parity(j, lambda cur: step(j, cur))
        return carry

    lax.fori_loop(1, i, body, 0)

    @pl.when(i > 0)
    def _():
        by_parity(i, last)
    lam = _lambda_value(lam_ref, lam_init)
    o = acc_s[...] / jnp.sum(l_s[...], axis=1, keepdims=True)
    o_ref[...] = _finish_head(o, tq, lam, lam_init, sg_ref[...], ga_ref[...]).astype(o_ref.dtype)


def _attn_prompt(q, kb, vb, rest, lamv, subln, *, batch, seq, tq, lam_init):
    assert tq & (tq - 1) == 0 and tq % LANES == 0 and seq % tq == 0
    t = batch * seq
    nq = seq // tq
    kern = functools.partial(_attn_prompt_kernel, tq=tq, lam_init=lam_init)
    return pl.pallas_call(
        kern,
        grid=(batch, HA, nq),
        in_specs=[pl.BlockSpec((4, DA), lambda b, h, i: (0, 0)),
                  pl.BlockSpec((tq, LANES), lambda b, h, i: (b * nq + i, h)),
                  pl.BlockSpec((seq, LANES), lambda b, h, i: (b, h)),
                  pl.BlockSpec((seq, LANES), lambda b, h, i: (b, h)),
                  pl.BlockSpec((tq, LANES), lambda b, h, i: (b * nq + i, h)),
                  pl.BlockSpec((1, LANES), lambda b, h, i: (0, 0))],
        out_specs=pl.BlockSpec((tq, LANES), lambda b, h, i: (b * nq + i, h)),
        out_shape=jax.ShapeDtypeStruct((t, W_GRP), BF16),
        scratch_shapes=[pltpu.VMEM((2 * tq, LANES), BF16),
                        pltpu.VMEM((2 * tq, tq), F32), pltpu.VMEM((2 * tq, tq), F32),
                        pltpu.VMEM((2 * tq, tq), BF16), pltpu.VMEM((2 * tq, tq), BF16),
                        pltpu.VMEM((2 * tq, LANES), F32), pltpu.VMEM((2 * tq, LANES), F32),
                        pltpu.VMEM((2 * tq, LANES), F32), pltpu.VMEM((2 * tq, LANES), F32)],
        compiler_params=pltpu.CompilerParams(
            dimension_semantics=("arbitrary", "arbitrary", "arbitrary"), vmem_limit_bytes=VMEM_LIMIT),
        name="attn_prompt",
    )(lamv, q, kb, vb, rest, subln)


def _attn_decode_kernel(pt_ref, lam_ref, q_ref, kn_ref, vn_ref, ga_ref, sg_ref, *refs,
                        npg, past_len, lam_init):
    del pt_ref
    k_refs = refs[:npg]
    v_refs = refs[npg:2 * npg]
    o_ref = refs[2 * npg]
    m_s, l_s, acc_s = refs[2 * npg + 1:]
    j = pl.program_id(1)
    nj = pl.num_programs(1)
    nq = q_ref.shape[0]
    rows = 2 * nq
    blk = npg * PAGE

    @pl.when(j == 0)
    def _():
        m_s[...] = jnp.full(m_s.shape, NEG_INF, F32)
        l_s[...] = jnp.zeros(l_s.shape, F32)
        acc_s[...] = jnp.zeros(acc_s.shape, F32)

    q2 = [_split_maps(q_ref[:, h * LANES:(h + 1) * LANES]) for h in range(HA)]
    rowi = lax.broadcasted_iota(jnp.int32, (HA * rows, 1), 0)
    slope_col = jnp.full((HA * rows, 1), _alibi_slope2(HA - 1), F32)
    for h in range(HA - 2, -1, -1):
        slope_col = jnp.where(rowi < (h + 1) * rows, _alibi_slope2(h), slope_col)

    def update(s, v_of_head):
        m_old = m_s[...]
        m_new = jnp.maximum(m_old, jnp.max(s, axis=1, keepdims=True))
        alpha = jnp.exp2(m_old - m_new)
        p = jnp.exp2(s - m_new)
        l_new = alpha * l_s[...] + jnp.sum(p, axis=1, keepdims=True)
        p = p.astype(BF16)
        pv = jnp.concatenate([_dot(p[h * rows:(h + 1) * rows], v_of_head(h)) for h in range(HA)], axis=0)
        return m_new, l_new, alpha * acc_s[...] + pv

    def k_past(h):
        return jnp.concatenate([r[pl.ds(h, PAGE, stride=HA), :] for r in k_refs], axis=0).astype(BF16)

    def v_past(h):
        return jnp.concatenate([r[pl.ds(h, PAGE, stride=HA), :] for r in v_refs], axis=0).astype(BF16)

    col = lax.broadcasted_iota(jnp.int32, (1, blk), 1)
    s = jnp.concatenate([_dot_nt(q2[h], k_past(h)) for h in range(HA)], axis=0)
    s = s + slope_col * (col + (j * blk - past_len)).astype(F32)
    m_new, l_new, acc_new = update(s, v_past)
    m_s[...] = m_new
    l_s[...] = l_new
    acc_s[...] = acc_new

    @pl.when(j == nj - 1)
    def _():
        lam = _lambda_value(lam_ref, lam_init)
        zpad = jnp.zeros((LANES - nq, LANES), F32)
        kn = lambda h: jnp.concatenate([kn_ref[:, h, :], zpad], axis=0).astype(BF16)
        vn = lambda h: jnp.concatenate([vn_ref[:, h, :], zpad], axis=0).astype(BF16)
        r = lax.broadcasted_iota(jnp.int32, (HA * rows, LANES), 0) & (nq - 1)
        c = lax.broadcasted_iota(jnp.int32, (HA * rows, LANES), 1)
        sn = jnp.concatenate([_dot_nt(q2[h], kn(h)) for h in range(HA)], axis=0)
        sn = jnp.where(c <= r, sn + slope_col * c.astype(F32), NEG_INF)
        _, l_fin, acc_fin = update(sn, vn)
        o = acc_fin / l_fin
        for h in range(HA):
            hs = slice(h * LANES, (h + 1) * LANES)
            o_ref[:, hs] = _finish_head(o[h * rows:(h + 1) * rows], nq, lam, lam_init, sg_ref[...],
                                        ga_ref[:, hs])


def _attn_decode(pt_flat, q, kn, vn, rest, lamv, subln, cache_k, cache_v, *, layer, dec_b, dec_t,
                 n_pages, npg, lam_init):
    assert dec_t & (dec_t - 1) == 0
    past_len = n_pages * PAGE
    kern = functools.partial(_attn_decode_kernel, npg=npg, past_len=past_len, lam_init=lam_init)

    def page_spec(i):
        return pl.BlockSpec((None, None, PAGE * HA, LANES),
                            lambda b, j, pt: (layer, pt[b * n_pages + j * npg + i], 0, 0))

    row_spec = pl.BlockSpec((dec_t, W_GRP), lambda b, j, pt: (b, 0))
    new_spec = pl.BlockSpec((dec_t, HA, LANES), lambda b, j, pt: (b, 0, 0))
    grid_spec = pltpu.PrefetchScalarGridSpec(
        num_scalar_prefetch=1,
        grid=(dec_b, n_pages // npg),
        in_specs=[pl.BlockSpec((4, DA), lambda b, j, pt: (0, 0)),
                  row_spec, new_spec, new_spec, row_spec,
                  pl.BlockSpec((1, LANES), lambda b, j, pt: (0, 0))]
                 + [page_spec(i) for i in range(npg)] + [page_spec(i) for i in range(npg)],
        out_specs=row_spec,
        scratch_shapes=[pltpu.VMEM((HA * 2 * dec_t, 1), F32),
                        pltpu.VMEM((HA * 2 * dec_t, 1), F32),
                        pltpu.VMEM((HA * 2 * dec_t, LANES), F32)],
    )
    return pl.pallas_call(
        kern,
        grid_spec=grid_spec,
        out_shape=jax.ShapeDtypeStruct((dec_b * dec_t, W_GRP), F32),
        compiler_params=pltpu.CompilerParams(dimension_semantics=("arbitrary", "arbitrary"),
                                             vmem_limit_bytes=VMEM_LIMIT),
        name="attn_decode",
    )(pt_flat, lamv, q, kn, vn, rest, subln, *([cache_k] * npg), *([cache_v] * npg))


def _mixer_kernel(rest_ref, ret0_ref, ssm0_ref, conv0_ref, glg_ref, glb_ref, ws_ref, bs_ref,
                  rgg_ref, rgb_ref, cw_ref, cb_ref, dtb_ref, alog_ref, dskip_ref, sng_ref,
                  *refs, seg_len, n_seg, emit_vbn):
    if emit_vbn:
        out_ref, ret_ref, ssm_ref, convn_ref, vbn_ref, xpad_s, dl_s = refs
    else:
        out_ref, ret_ref, ssm_ref, convn_ref, xpad_s, dl_s = refs
        vbn_ref = None
    L, G = seg_len, n_seg
    R = TILE_ROWS
    lg = int(math.log2(L))
    c = pl.program_id(1)

    @pl.when(c == 0)
    def _():
        ret_ref[...] = ret0_ref[...]
        ssm_ref[...] = ssm0_ref[...]
        xpad_s[:, 8 - (CONV_W - 1):8, :] = conv0_ref[...]

    tt = lax.broadcasted_iota(jnp.int32, (R, R), 0)
    ss = lax.broadcasted_iota(jnp.int32, (R, R), 1)
    same = (tt >> lg) == (ss >> lg)
    mask = same & (ss <= tt)
    diff = jnp.maximum(tt - ss, 0).astype(F32)
    trow = lax.broadcasted_iota(jnp.int32, (R, 1), 0)
    tpos = (trow & (L - 1)).astype(F32)
    rseg = trow >> lg

    def over_segments(body, init):
        carry = init
        for j in range(G):
            carry = body(j, None if G == 1 else rseg == j, carry)
        return carry

    def rows_of(rm, x):
        return x if rm is None else jnp.where(rm, x, 0.0)

    vb = rest_ref[:, R_VB:R_VB + W_GRP]
    mu = jnp.mean(vb, axis=1, keepdims=True)
    vc_ = vb - mu
    var = jnp.mean(vc_ * vc_, axis=1, keepdims=True)
    vbn = vc_ * lax.rsqrt(var + LN_EPS) * glg_ref[...] + glb_ref[...]
    if emit_vbn:
        vbn_ref[...] = vbn
    for g in range(GB):
        gs = slice(g * LANES, (g + 1) * LANES)
        w = jnp.where(mask, ws_ref[g], 0.0).astype(BF16)
        mixed = _dot(w, vbn[:, gs].astype(BF16)) + bs_ref[:, g:g + 1]
        ub = rest_ref[:, R_UB + g * LANES:R_UB + (g + 1) * LANES]
        gb = rest_ref[:, R_GB + g * LANES:R_GB + (g + 1) * LANES]
        out_ref[:, gs] = (ub * mixed * _silu(gb)).astype(out_ref.dtype)

    for h in range(HC):
        log_g = math.log(1.0 - 2.0 ** (-5.0 - h))
        q = rest_ref[:, R_QC + h * DKC:R_QC + (h + 1) * DKC].astype(BF16)
        k = rest_ref[:, R_KC + h * DKC:R_KC + (h + 1) * DKC] * (DKC ** -0.5)
        v = rest_ref[:, R_VC + h * DVC:R_VC + (h + 1) * DVC]
        decay = jnp.where(mask, jnp.exp(log_g * diff), 0.0)
        scores = _dot_nt(q, k.astype(BF16)) * decay
        o = _dot(scores.astype(BF16), v.astype(BF16))
        row_scale = jnp.exp(log_g * (tpos + 1.0))
        vw = (v * jnp.exp(log_g * (L - 1.0 - tpos))).astype(BF16)
        g_len = math.exp(log_g * L)

        def ret_body(j, rm, o_state, h=h, q=q, k=k, vw=vw, g_len=g_len):
            s0 = ret_ref[j, h]
            o_state = o_state + rows_of(rm, _dot(q, s0.astype(BF16)))
            ret_ref[j, h] = g_len * s0 + _dot_tn(rows_of(rm, k).astype(BF16), vw)
            return o_state

        o = o + over_segments(ret_body, jnp.zeros((R, DVC), F32)) * row_scale
        mu = jnp.mean(o, axis=1, keepdims=True)
        oc = o - mu
        var = jnp.mean(oc * oc, axis=1, keepdims=True)
        hs = slice(h * DVC, (h + 1) * DVC)
        oc = oc * lax.rsqrt(var + LN_EPS) * rgg_ref[:, hs] + rgb_ref[:, hs]
        gc = rest_ref[:, R_GC + h * DVC:R_GC + (h + 1) * DVC]
        out_ref[:, W_GRP + h * DVC:W_GRP + (h + 1) * DVC] = (oc * _silu(gc)).astype(out_ref.dtype)

    x_cur = rest_ref[:, R_XBC:R_XBC + CONV_DIM]
    xpad_s[:, 8:8 + L, :] = x_cur.reshape(G, L, CONV_DIM)
    conv = cb_ref[...] + cw_ref[CONV_W - 1:CONV_W, :] * x_cur
    for kk in range(1, CONV_W):
        shifted = xpad_s[:, 8 - kk:8 - kk + L, :].reshape(R, CONV_DIM)
        conv = conv + cw_ref[CONV_W - 1 - kk:CONV_W - kk, :] * shifted
    convn_ref[...] = xpad_s[:, L + 8 - (CONV_W - 1):L + 8, :]
    xpad_s[:, 0:8, :] = xpad_s[:, L:L + 8, :]
    conv = _silu(conv)
    xd = conv[:, :W_GRP]
    bd = conv[:, W_GRP:W_GRP + NG * NS]
    cd = conv[:, W_GRP + NG * NS:]

    zt = rest_ref[:, R_DT:R_DT + LANES] + dtb_ref[...]
    dt = jnp.maximum(zt, 0.0) + jnp.log1p(jnp.exp(-jnp.abs(zt)))
    dta = dt * (-jnp.exp(alog_ref[...]))
    cum = jnp.dot(mask.astype(F32), dta, precision=lax.Precision.HIGHEST, preferred_element_type=F32)
    cum_last = jnp.dot(same.astype(F32), dta, precision=lax.Precision.HIGHEST,
                       preferred_element_type=F32)
    cum_t = cum.T
    dt_t = dt.T
    ecum = jnp.exp(cum)
    wlast = jnp.exp(cum_last - cum) * dt
    dl_s[...] = jnp.exp(cum_last)

    yd_parts = []
    for grp in range(NG):
        bg = bd[:, grp * NS:(grp + 1) * NS].astype(BF16)
        cg = cd[:, grp * NS:(grp + 1) * NS].astype(BF16)
        cb = _dot_nt(cg, bg)
        for hh in range(HD // NG):
            h = grp * (HD // NG) + hh
            seg = cum[:, h:h + 1] - cum_t[h:h + 1, :]
            lmat = jnp.where(mask, jnp.exp(jnp.where(mask, seg, 0.0)), 0.0)
            scores = cb * lmat * dt_t[h:h + 1, :]
            x_h = xd[:, h * PD:(h + 1) * PD]
            y = _dot(scores.astype(BF16), x_h.astype(BF16))
            xw = x_h * wlast[:, h:h + 1]

            def ssd_body(j, rm, y_state, h=h, cg=cg, bg=bg, xw=xw):
                h0 = ssm_ref[j, h]
                y_state = y_state + rows_of(rm, _dot_nt(cg, h0.astype(BF16)))
                dec = dl_s[j * L:j * L + 1, h:h + 1]
                ssm_ref[j, h] = dec * h0 + _dot_tn(rows_of(rm, xw).astype(BF16), bg)
                return y_state

            y = y + over_segments(ssd_body, jnp.zeros((R, PD), F32)) * ecum[:, h:h + 1]
            yd_parts.append(y + dskip_ref[:, h * PD:(h + 1) * PD] * x_h)
    yd = jnp.concatenate(yd_parts, axis=1)
    z = yd * _silu(rest_ref[:, R_ZD:R_ZD + W_GRP])
    out_ref[:, 2 * W_GRP:3 * W_GRP] = (
        z * lax.rsqrt(jnp.mean(z * z, axis=1, keepdims=True) + LN_EPS) * sng_ref[...]
    ).astype(out_ref.dtype)


def _mixers(rest, ret0, ssm0, conv0, p, *, seg_len, n_chunks, emit_vbn):
    t = rest.shape[0]
    n_seq = ret0.shape[0]
    G = TILE_ROWS // seg_len
    n_grp = n_seq // G
    kern = functools.partial(_mixer_kernel, seg_len=seg_len, n_seg=G, emit_vbn=emit_vbn)

    def full(shape):
        nd = len(shape)
        return pl.BlockSpec(shape, lambda g, c, _nd=nd: (0,) * _nd)

    in_specs = [pl.BlockSpec((TILE_ROWS, REST_COLS), lambda g, c: (g * n_chunks + c, 0)),
                pl.BlockSpec((G, HC, DKC, DVC), lambda g, c: (g, 0, 0, 0)),
                pl.BlockSpec((G, HD, PD, NS), lambda g, c: (g, 0, 0, 0)),
                pl.BlockSpec((G, CONV_W - 1, CONV_DIM), lambda g, c: (g, 0, 0)),
                full((1, W_GRP)), full((1, W_GRP)), full((GB, TILE_ROWS, TILE_ROWS)),
                full((TILE_ROWS, GB)), full((1, W_GRP)), full((1, W_GRP)),
                full((CONV_W, CONV_DIM)), full((1, CONV_DIM)), full((1, LANES)), full((1, LANES)),
                full((1, W_GRP)), full((1, W_GRP))]
    out_specs = [pl.BlockSpec((TILE_ROWS, 3 * W_GRP), lambda g, c: (g * n_chunks + c, 0)),
                 pl.BlockSpec((G, HC, DKC, DVC), lambda g, c: (g, 0, 0, 0)),
                 pl.BlockSpec((G, HD, PD, NS), lambda g, c: (g, 0, 0, 0)),
                 pl.BlockSpec((G, CONV_W - 1, CONV_DIM), lambda g, c: (g, 0, 0))]
    out_shape = [jax.ShapeDtypeStruct((t, 3 * W_GRP), BF16),
                 jax.ShapeDtypeStruct((n_seq, HC, DKC, DVC), F32),
                 jax.ShapeDtypeStruct((n_seq, HD, PD, NS), F32),
                 jax.ShapeDtypeStruct((n_seq, CONV_W - 1, CONV_DIM), F32)]
    if emit_vbn:
        out_specs.append(pl.BlockSpec((TILE_ROWS, W_GRP), lambda g, c: (g * n_chunks + c, 0)))
        out_shape.append(jax.ShapeDtypeStruct((t, W_GRP), F32))
    return pl.pallas_call(
        kern,
        grid=(n_grp, n_chunks),
        in_specs=in_specs,
        out_specs=out_specs,
        out_shape=out_shape,
        scratch_shapes=[pltpu.VMEM((G, seg_len + 8, CONV_DIM), F32),
                        pltpu.VMEM((TILE_ROWS, LANES), F32)],
        compiler_params=pltpu.CompilerParams(dimension_semantics=("arbitrary", "arbitrary"),
                                             vmem_limit_bytes=VMEM_LIMIT),
        name="mixers",
    )(rest, ret0, ssm0, conv0, p["glg"], p["glb"], p["ws"], p["bs"], p["rgg"], p["rgb"],
      p["cw"], p["cb"], p["dtb"], p["alog"], p["dskip"], p["sng"])


def _outproj_kernel(a_ref, m_ref, x_ref, w_ref, g_ref, b_ref, y_ref, *, alpha):
    hproj = _dot(a_ref[...].astype(BF16), w_ref[0:W_GRP, :])
    hproj = hproj + _dot(m_ref[...], w_ref[W_GRP:, :])
    z = alpha * x_ref[...] + hproj
    mu = jnp.mean(z, axis=1, keepdims=True)
    zc = z - mu
    var = jnp.mean(zc * zc, axis=1, keepdims=True)
    y_ref[...] = zc * lax.rsqrt(var + LN_EPS) * g_ref[...] + b_ref[...]


def _outproj(out_a, out_bcd, x2, w_out, ln_g, ln_b, tm, alpha):
    t, d = x2.shape
    kern = functools.partial(_outproj_kernel, alpha=alpha)
    return pl.pallas_call(
        kern,
        grid=(t // tm,),
        in_specs=[pl.BlockSpec((tm, W_GRP), lambda i: (i, 0)),
                  pl.BlockSpec((tm, 3 * W_GRP), lambda i: (i, 0)),
                  pl.BlockSpec((tm, d), lambda i: (i, 0)),
                  pl.BlockSpec((4 * W_GRP, d), lambda i: (0, 0)),
                  pl.BlockSpec((1, d), lambda i: (0, 0)),
                  pl.BlockSpec((1, d), lambda i: (0, 0))],
        out_specs=pl.BlockSpec((tm, d), lambda i: (i, 0)),
        out_shape=jax.ShapeDtypeStruct((t, d), F32),
        compiler_params=pltpu.CompilerParams(dimension_semantics=("arbitrary",),
                                             vmem_limit_bytes=VMEM_LIMIT),
        name="outproj",
    )(out_a, out_bcd, x2, w_out, ln_g, ln_b)


def _layer_params(l, seg_len, gmlp_ln_g, gmlp_ln_b, gmlp_ws, gmlp_bs, ret_gn_g, ret_gn_b, conv_w,
                  conv_b, dt_bias, a_log, d_skip, ssd_norm_g):
    G = TILE_ROWS // seg_len
    pad8 = lambda v: jnp.pad(v.astype(F32), (0, LANES - HD)).reshape(1, LANES)
    return {
        "glg": gmlp_ln_g[l].reshape(1, W_GRP), "glb": gmlp_ln_b[l].reshape(1, W_GRP),
        "ws": jnp.tile(gmlp_ws[l][:, :seg_len, :seg_len], (1, G, G)),
        "bs": jnp.tile(gmlp_bs[l][:, :seg_len].T, (G, 1)),
        "rgg": ret_gn_g[l].reshape(1, W_GRP), "rgb": ret_gn_b[l].reshape(1, W_GRP),
        "cw": conv_w[l], "cb": conv_b[l].reshape(1, CONV_DIM),
        "dtb": pad8(dt_bias[l]), "alog": pad8(a_log[l]),
        "dskip": jnp.repeat(d_skip[l].astype(F32), PD).reshape(1, W_GRP),
        "sng": ssd_norm_g[l].reshape(1, W_GRP),
    }


def kernel(x_prompt, x_sample, cache_k, cache_v, page_table, state_ret, state_ssm, state_conv, w_in, w_out, ln_g, ln_b, lam_q1, lam_k1, lam_q2, lam_k2, subln_g, gmlp_ln_g, gmlp_ln_b, gmlp_ws, gmlp_bs, ret_gn_g, ret_gn_b, conv_w, conv_b, dt_bias, a_log, d_skip, ssd_norm_g):
    depth = w_in.shape[0]
    batch, seq, d_model = x_prompt.shape
    dec_b, dec_t, _ = x_sample.shape
    n_pages = page_table.shape[1]
    in_cols = w_in.shape[2]
    alpha = (2.0 * depth) ** 0.25
    assert in_cols == QKV_COLS + R_DT + HD and seq % TILE_ROWS == 0 and TILE_ROWS % dec_t == 0
    assert (dec_b * dec_t) % TILE_ROWS == 0

    w_in_p = jnp.pad(w_in.astype(BF16), ((0, 0), (0, 0), (0, QKV_COLS + REST_COLS - in_cols)))
    w_out_b = w_out.astype(BF16)
    ck = cache_k.reshape(cache_k.shape[0], cache_k.shape[1], PAGE * HA, LANES)
    cv = cache_v.reshape(cache_v.shape[0], cache_v.shape[1], PAGE * HA, LANES)
    pt_flat = page_table.reshape(-1).astype(jnp.int32)
    zeros_ret = jnp.zeros((batch, HC, DKC, DVC), F32)
    zeros_ssm = jnp.zeros((batch, HD, PD, NS), F32)
    zeros_conv = jnp.zeros((batch, CONV_W - 1, CONV_DIM), F32)
    mix_args = (gmlp_ln_g, gmlp_ln_b, gmlp_ws, gmlp_bs, ret_gn_g, ret_gn_b, conv_w, conv_b, dt_bias,
                a_log, d_skip, ssd_norm_g)

    tp = batch * seq
    tsm = dec_b * dec_t
    tq = min(256, seq)
    npg = 8 if n_pages % 8 == 0 else 1
    yp = x_prompt.reshape(tp, d_model)
    ys = x_sample.reshape(tsm, d_model)
    outs = [[] for _ in range(11)]
    for l in range(depth):
        lam_init = 0.8 - 0.6 * math.exp(-0.3 * l)
        lamv = jnp.stack([lam_q1[l], lam_k1[l], lam_q2[l], lam_k2[l]]).astype(F32)
        subln = subln_g[l].reshape(1, LANES)
        lng = ln_g[l].reshape(1, d_model)
        lnb = ln_b[l].reshape(1, d_model)

        pp = _layer_params(l, TILE_ROWS, *mix_args)
        q, k, v, kb, vb, rest = _inproj(yp, w_in_p[l], 256, with_bf16_kv=True)
        oa = _attn_prompt(q, kb, vb, rest, lamv, subln, batch=batch, seq=seq, tq=tq, lam_init=lam_init)
        obcd, rp, hp, cp = _mixers(rest, zeros_ret, zeros_ssm, zeros_conv, pp, seg_len=TILE_ROWS,
                                   n_chunks=seq // TILE_ROWS, emit_vbn=False)
        yp = _outproj(oa, obcd, yp, w_out_b[l], lng, lnb, 256, alpha)

        ps = _layer_params(l, dec_t, *mix_args)
        qs, ksn, vsn, rest_s = _inproj(ys, w_in_p[l], min(256, tsm), with_bf16_kv=False)
        oas = _attn_decode(pt_flat, qs, ksn, vsn, rest_s, lamv, subln, ck, cv, layer=l, dec_b=dec_b,
                           dec_t=dec_t, n_pages=n_pages, npg=npg, lam_init=lam_init)
        obcd_s, rs, hs, cs, gs = _mixers(rest_s, state_ret[l], state_ssm[l], state_conv[l], ps,
                                         seg_len=dec_t, n_chunks=1, emit_vbn=True)
        ys = _outproj(oas, obcd_s, ys, w_out_b[l], lng, lnb, min(256, tsm), alpha)

        for lst, val in zip(outs, (k.reshape(batch, seq, HA, 2 * DA), v.reshape(batch, seq, HA, 2 * DA),
                                   ksn.reshape(dec_b, dec_t, HA, 2 * DA),
                                   vsn.reshape(dec_b, dec_t, HA, 2 * DA),
                                   rp, rs, hp, hs, cp, cs, gs.reshape(dec_b, dec_t, W_GRP))):
            lst.append(val)
    stacked = [jnp.stack(o) for o in outs]
    return (yp.reshape(batch, seq, d_model), ys.reshape(dec_b, dec_t, d_model), *stacked)
```

```python
import functools
import math

import jax
import jax.numpy as jnp
from jax import lax
from jax.experimental import pallas as pl
from jax.experimental.pallas import tpu as pltpu

F32 = jnp.float32
BF16 = jnp.bfloat16

LANES = 128
TILE_ROWS = 128
HA = 4
DA = 64
GB = 4
HC = 4
DKC = 64
DVC = 128
HD = 8
PD = 64
NS = 128
NG = 2
CONV_W = 4
W_GRP = 512
CONV_DIM = W_GRP + 2 * NG * NS
LN_EPS = 1e-5
NEG_INF = -1e30
PAGE = 128
VMEM_LIMIT = 56 * 1024 * 1024
LOG2E = math.log2(math.e)
Q_SCALE = DA ** -0.5 * LOG2E

R_GA, R_UB, R_VB, R_GB = 0, 512, 1024, 1536
R_QC, R_KC, R_VC, R_GC = 2048, 2304, 2560, 3072
R_ZD, R_XBC, R_DT = 3584, 4096, 5120
REST_COLS = 5248
QKV_COLS = 3 * W_GRP


def _dot(a, b):
    return jnp.dot(a, b, preferred_element_type=F32)


def _dot_nt(a, b):
    return lax.dot_general(a, b, (((1,), (1,)), ((), ())), preferred_element_type=F32)


def _dot_tn(a, b):
    return lax.dot_general(a, b, (((0,), (0,)), ((), ())), preferred_element_type=F32)


def _silu(x):
    return x * (1.0 / (1.0 + jnp.exp(-x)))


def _alibi_slope2(h):
    return 2.0 ** (-8.0 * (h + 1) / HA) * LOG2E


def _inproj_kernel(x_ref, w_ref, *out_refs, with_bf16_kv):
    if with_bf16_kv:
        q_ref, k_ref, v_ref, kb_ref, vb_ref, r_ref = out_refs
    else:
        q_ref, k_ref, v_ref, r_ref = out_refs
    xb = x_ref[...].astype(BF16)
    q_ref[...] = (_dot(xb, w_ref[:, 0:W_GRP]) * Q_SCALE).astype(q_ref.dtype)
    k = _dot(xb, w_ref[:, W_GRP:2 * W_GRP])
    v = _dot(xb, w_ref[:, 2 * W_GRP:3 * W_GRP])
    for h in range(HA):
        k_ref[:, h, :] = k[:, h * LANES:(h + 1) * LANES]
        v_ref[:, h, :] = v[:, h * LANES:(h + 1) * LANES]
    if with_bf16_kv:
        kb_ref[...] = k.astype(BF16)
        vb_ref[...] = v.astype(BF16)
    r_ref[...] = _dot(xb, w_ref[:, QKV_COLS:])


def _inproj(x2, w_pad, layer, tm, *, with_bf16_kv):
    t, d = x2.shape
    ncols = w_pad.shape[2]
    row = lambda i: (i, 0)
    out_specs = [pl.BlockSpec((tm, W_GRP), row),
                 pl.BlockSpec((tm, HA, LANES), lambda i: (i, 0, 0)),
                 pl.BlockSpec((tm, HA, LANES), lambda i: (i, 0, 0))]
    out_shape = [jax.ShapeDtypeStruct((t, W_GRP), BF16 if with_bf16_kv else F32),
                 jax.ShapeDtypeStruct((t, HA, LANES), F32),
                 jax.ShapeDtypeStruct((t, HA, LANES), F32)]
    if with_bf16_kv:
        out_specs += [pl.BlockSpec((tm, W_GRP), row), pl.BlockSpec((tm, W_GRP), row)]
        out_shape += [jax.ShapeDtypeStruct((t, W_GRP), BF16), jax.ShapeDtypeStruct((t, W_GRP), BF16)]
    out_specs.append(pl.BlockSpec((tm, REST_COLS), row))
    out_shape.append(jax.ShapeDtypeStruct((t, REST_COLS), F32))
    return pl.pallas_call(
        functools.partial(_inproj_kernel, with_bf16_kv=with_bf16_kv),
        grid=(t // tm,),
        in_specs=[pl.BlockSpec((tm, d), row),
                  pl.BlockSpec((None, d, ncols), lambda i: (layer, 0, 0), pipeline_mode=pl.Buffered(1))],
        out_specs=out_specs,
        out_shape=out_shape,
        compiler_params=pltpu.CompilerParams(dimension_semantics=("arbitrary",),
                                             vmem_limit_bytes=VMEM_LIMIT),
        name="inproj",
    )(x2, w_pad)


def _split_maps(q):
    lane = lax.broadcasted_iota(jnp.int32, q.shape, 1)
    zero = jnp.zeros_like(q)
    q1 = jnp.where(lane < DA, q, zero)
    q2 = jnp.where(lane >= DA, q, zero)
    return jnp.concatenate([q1, q2], axis=0).astype(BF16)


def _lambda_value(lam_ref, lam_init):
    lv = lam_ref[...]
    a = jnp.sum(lv[0:1, :] * lv[1:2, :], axis=1, keepdims=True)
    b = jnp.sum(lv[2:3, :] * lv[3:4, :], axis=1, keepdims=True)
    return jnp.exp(a) - jnp.exp(b) + lam_init


def _finish_head(o, n, lam, lam_init, sg, ga):
    out = o[:n] - lam * o[n:]
    out = out * lax.rsqrt(jnp.mean(out * out, axis=1, keepdims=True) + LN_EPS) * sg * (1.0 - lam_init)
    return out * _silu(ga)


def _attn_prompt_kernel(lam_ref, q_ref, k_ref, v_ref, ga_ref, sg_ref, o_ref,
                        q2_s, s0_s, s1_s, p0_s, p1_s, m_s, a_s, l_s, acc_s, *, tq, nq, lam_init):
    h = pl.program_id(1)
    slope = (jnp.where(h == 0, 2.0 ** -2, jnp.where(h == 1, 2.0 ** -4,
                       jnp.where(h == 2, 2.0 ** -6, 2.0 ** -8))) * LOG2E).astype(F32)
    s_bufs = (s0_s, s1_s)
    p_bufs = (p0_s, p1_s)
    n_tiles = nq * (nq + 1) // 2
    for i in range(nq):
        q2_s[i] = _split_maps(q_ref[i * tq:(i + 1) * tq, :])
    m_s[...] = jnp.full(m_s.shape, NEG_INF, F32)
    l_s[...] = jnp.zeros(l_s.shape, F32)
    acc_s[...] = jnp.zeros(acc_s.shape, F32)
    col = lax.broadcasted_iota(jnp.int32, (1, tq), 1)

    def tile(ref, j):
        return ref[pl.ds(pl.multiple_of(j * tq, tq), tq), :]

    def qk(i, j, slot):
        s_bufs[slot][...] = _dot_nt(q2_s[i], tile(k_ref, j))

    def pv(i, j, slot):
        acc_s[i] = a_s[...] * acc_s[i] + _dot(p_bufs[slot][...], tile(v_ref, j))

    def softmax(i, j, slot, masked):
        bias = slope * (col + (j - i) * tq).astype(F32)
        if masked:
            r = lax.broadcasted_iota(jnp.int32, (2 * tq, LANES), 0) & (tq - 1)
            c = lax.broadcasted_iota(jnp.int32, (2 * tq, LANES), 1)
        sb = []
        for cb in range(tq // LANES):
            cols = slice(cb * LANES, (cb + 1) * LANES)
            x = s_bufs[slot][:, cols] + bias[:, cols]
            if masked:
                x = jnp.where(c + cb * LANES <= r, x, NEG_INF)
            sb.append(x)
        m_old = m_s[i]
        m_new = jnp.maximum(m_old, jnp.max(functools.reduce(jnp.maximum, sb), axis=1, keepdims=True))
        alpha = jnp.exp2(m_old - m_new)
        pb = [jnp.exp2(x - m_new) for x in sb]
        l_s[i] = alpha * l_s[i] + functools.reduce(jnp.add, pb)
        for cb in range(tq // LANES):
            p_bufs[slot][:, cb * LANES:(cb + 1) * LANES] = pb[cb].astype(BF16)
        m_s[i] = m_new
        a_s[...] = alpha

    def step(i, j, cur, masked):
        last_of_row = j == i
        qk(jnp.where(last_of_row, i + 1, i), jnp.where(last_of_row, 0, j + 1), 1 - cur)
        first_of_row = j == 0
        pv(jnp.where(first_of_row, i - 1, i), jnp.where(first_of_row, i - 1, j - 1), 1 - cur)
        softmax(i, j, cur, masked)

    qk(0, 0, 0)
    qk(1, 0, 1)
    softmax(0, 0, 0, True)

    def body(t, carry):
        i, j = carry
        for cur in (0, 1):
            for masked in (False, True):
                on_diagonal = (j == i) if masked else (j != i)
                pl.when(((t & 1) == cur) & on_diagonal)(functools.partial(step, i, j, cur, masked))
        last_of_row = j == i
        return jnp.where(last_of_row, i + 1, i), jnp.where(last_of_row, 0, j + 1)

    lax.fori_loop(1, n_tiles - 1, body, (jnp.int32(1), jnp.int32(0)))

    cur = (n_tiles - 1) & 1
    pv(nq - 1, nq - 2, 1 - cur)
    softmax(nq - 1, nq - 1, cur, True)
    pv(nq - 1, nq - 1, cur)

    lam = _lambda_value(lam_ref, lam_init)
    for i in range(nq):
        rows = slice(i * tq, (i + 1) * tq)
        o = acc_s[i] / jnp.sum(l_s[i], axis=1, keepdims=True)
        o_ref[rows, :] = _finish_head(o, tq, lam, lam_init, sg_ref[...], ga_ref[rows, :]).astype(o_ref.dtype)


def _attn_prompt(q, kb, vb, rest, lamv, subln, *, batch, seq, tq, lam_init):
    assert tq & (tq - 1) == 0 and tq % LANES == 0 and seq % tq == 0 and seq // tq >= 2
    t = batch * seq
    nq = seq // tq
    kern = functools.partial(_attn_prompt_kernel, tq=tq, nq=nq, lam_init=lam_init)
    head_block = pl.BlockSpec((seq, LANES), lambda b, h: (b, h))
    return pl.pallas_call(
        kern,
        grid=(batch, HA),
        in_specs=[pl.BlockSpec((4, DA), lambda b, h: (0, 0)),
                  head_block, head_block, head_block, head_block,
                  pl.BlockSpec((1, LANES), lambda b, h: (0, 0))],
        out_specs=head_block,
        out_shape=jax.ShapeDtypeStruct((t, W_GRP), BF16),
        scratch_shapes=[pltpu.VMEM((nq, 2 * tq, LANES), BF16),
                        pltpu.VMEM((2 * tq, tq), F32), pltpu.VMEM((2 * tq, tq), F32),
                        pltpu.VMEM((2 * tq, tq), BF16), pltpu.VMEM((2 * tq, tq), BF16),
                        pltpu.VMEM((nq, 2 * tq, LANES), F32), pltpu.VMEM((2 * tq, LANES), F32),
                        pltpu.VMEM((nq, 2 * tq, LANES), F32), pltpu.VMEM((nq, 2 * tq, LANES), F32)],
        compiler_params=pltpu.CompilerParams(
            dimension_semantics=("arbitrary", "arbitrary"), vmem_limit_bytes=VMEM_LIMIT),
        name="attn_prompt",
    )(lamv, q, kb, vb, rest, subln)


def _attn_decode_kernel(pt_ref, lam_ref, q_ref, kn_ref, vn_ref, ga_ref, sg_ref, *refs,
                        npg, past_len, lam_init):
    del pt_ref
    k_refs = refs[:npg]
    v_refs = refs[npg:2 * npg]
    o_ref = refs[2 * npg]
    m_s, l_s, acc_s = refs[2 * npg + 1:]
    j = pl.program_id(1)
    nj = pl.num_programs(1)
    nq = q_ref.shape[0]
    rows = 2 * nq
    blk = npg * PAGE

    @pl.when(j == 0)
    def _():
        m_s[...] = jnp.full(m_s.shape, NEG_INF, F32)
        l_s[...] = jnp.zeros(l_s.shape, F32)
        acc_s[...] = jnp.zeros(acc_s.shape, F32)

    q2 = [_split_maps(q_ref[:, h * LANES:(h + 1) * LANES]) for h in range(HA)]
    rowi = lax.broadcasted_iota(jnp.int32, (HA * rows, 1), 0)
    slope_col = jnp.full((HA * rows, 1), _alibi_slope2(HA - 1), F32)
    for h in range(HA - 2, -1, -1):
        slope_col = jnp.where(rowi < (h + 1) * rows, _alibi_slope2(h), slope_col)

    def update(s, v_of_head):
        m_old = m_s[...]
        m_new = jnp.maximum(m_old, jnp.max(s, axis=1, keepdims=True))
        alpha = jnp.exp2(m_old - m_new)
        p = jnp.exp2(s - m_new)
        l_new = alpha * l_s[...] + jnp.sum(p, axis=1, keepdims=True)
        p = p.astype(BF16)
        pv = jnp.concatenate([_dot(p[h * rows:(h + 1) * rows], v_of_head(h)) for h in range(HA)], axis=0)
        return m_new, l_new, alpha * acc_s[...] + pv

    def k_past(h):
        return jnp.concatenate([r[pl.ds(h, PAGE, stride=HA), :] for r in k_refs], axis=0).astype(BF16)

    def v_past(h):
        return jnp.concatenate([r[pl.ds(h, PAGE, stride=HA), :] for r in v_refs], axis=0).astype(BF16)

    col = lax.broadcasted_iota(jnp.int32, (1, blk), 1)
    s = jnp.concatenate([_dot_nt(q2[h], k_past(h)) for h in range(HA)], axis=0)
    s = s + slope_col * (col + (j * blk - past_len)).astype(F32)
    m_new, l_new, acc_new = update(s, v_past)
    m_s[...] = m_new
    l_s[...] = l_new
    acc_s[...] = acc_new

    @pl.when(j == nj - 1)
    def _():
        lam = _lambda_value(lam_ref, lam_init)
        zpad = jnp.zeros((LANES - nq, LANES), F32)
        kn = lambda h: jnp.concatenate([kn_ref[:, h, :], zpad], axis=0).astype(BF16)
        vn = lambda h: jnp.concatenate([vn_ref[:, h, :], zpad], axis=0).astype(BF16)
        r = lax.broadcasted_iota(jnp.int32, (HA * rows, LANES), 0) & (nq - 1)
        c = lax.broadcasted_iota(jnp.int32, (HA * rows, LANES), 1)
        sn = jnp.concatenate([_dot_nt(q2[h], kn(h)) for h in range(HA)], axis=0)
        sn = jnp.where(c <= r, sn + slope_col * c.astype(F32), NEG_INF)
        _, l_fin, acc_fin = update(sn, vn)
        o = acc_fin / l_fin
        for h in range(HA):
            hs = slice(h * LANES, (h + 1) * LANES)
            o_ref[:, hs] = _finish_head(o[h * rows:(h + 1) * rows], nq, lam, lam_init, sg_ref[...],
                                        ga_ref[:, hs])


def _attn_decode(pt_flat, q, kn, vn, rest, lamv, subln, cache_k, cache_v, *, layer, dec_b, dec_t,
                 n_pages, npg, lam_init):
    assert dec_t & (dec_t - 1) == 0
    past_len = n_pages * PAGE
    kern = functools.partial(_attn_decode_kernel, npg=npg, past_len=past_len, lam_init=lam_init)

    def page_spec(i):
        return pl.BlockSpec((None, None, PAGE * HA, LANES),
                            lambda b, j, pt: (layer, pt[b * n_pages + j * npg + i], 0, 0))

    row_spec = pl.BlockSpec((dec_t, W_GRP), lambda b, j, pt: (b, 0))
    new_spec = pl.BlockSpec((dec_t, HA, LANES), lambda b, j, pt: (b, 0, 0))
    grid_spec = pltpu.PrefetchScalarGridSpec(
        num_scalar_prefetch=1,
        grid=(dec_b, n_pages // npg),
        in_specs=[pl.BlockSpec((4, DA), lambda b, j, pt: (0, 0)),
                  row_spec, new_spec, new_spec, row_spec,
                  pl.BlockSpec((1, LANES), lambda b, j, pt: (0, 0))]
                 + [page_spec(i) for i in range(npg)] + [page_spec(i) for i in range(npg)],
        out_specs=row_spec,
        scratch_shapes=[pltpu.VMEM((HA * 2 * dec_t, 1), F32),
                        pltpu.VMEM((HA * 2 * dec_t, 1), F32),
                        pltpu.VMEM((HA * 2 * dec_t, LANES), F32)],
    )
    return pl.pallas_call(
        kern,
        grid_spec=grid_spec,
        out_shape=jax.ShapeDtypeStruct((dec_b * dec_t, W_GRP), F32),
        compiler_params=pltpu.CompilerParams(dimension_semantics=("arbitrary", "arbitrary"),
                                             vmem_limit_bytes=VMEM_LIMIT),
        name="attn_decode",
    )(pt_flat, lamv, q, kn, vn, rest, subln, *([cache_k] * npg), *([cache_v] * npg))


def _mixer_kernel(rest_ref, ret0_ref, ssm0_ref, conv0_ref, glg_ref, glb_ref, ws_ref, bs_ref,
                  rgg_ref, rgb_ref, cw_ref, cb_ref, dtb_ref, alog_ref, dskip_ref, sng_ref,
                  *refs, seg_len, n_seg, emit_vbn):
    if emit_vbn:
        out_ref, ret_ref, ssm_ref, convn_ref, vbn_ref, xpad_s, dl_s, wsm_s, dec_s = refs
    else:
        out_ref, ret_ref, ssm_ref, convn_ref, xpad_s, dl_s, wsm_s, dec_s = refs
        vbn_ref = None
    L, G = seg_len, n_seg
    R = TILE_ROWS
    lg = int(math.log2(L))
    c = pl.program_id(1)
    ret_log_g = [math.log(1.0 - 2.0 ** (-5.0 - h)) for h in range(HC)]

    tt = lax.broadcasted_iota(jnp.int32, (R, R), 0)
    ss = lax.broadcasted_iota(jnp.int32, (R, R), 1)
    same = (tt >> lg) == (ss >> lg)
    mask = same & (ss <= tt)

    @pl.when(c == 0)
    def _():
        ret_ref[...] = ret0_ref[...]
        ssm_ref[...] = ssm0_ref[...]
        xpad_s[:, 8 - (CONV_W - 1):8, :] = conv0_ref[...]
        diff = jnp.maximum(tt - ss, 0).astype(F32)
        for g in range(GB):
            wsm_s[g] = jnp.where(mask, ws_ref[g], 0.0).astype(BF16)
        for h in range(HC):
            dec_s[h] = jnp.where(mask, jnp.exp(ret_log_g[h] * diff), 0.0)

    trow = lax.broadcasted_iota(jnp.int32, (R, 1), 0)
    tpos = (trow & (L - 1)).astype(F32)
    rseg = trow >> lg

    def over_segments(body, init):
        carry = init
        for j in range(G):
            carry = body(j, None if G == 1 else rseg == j, carry)
        return carry

    def rows_of(rm, x):
        return x if rm is None else jnp.where(rm, x, 0.0)

    vb = rest_ref[:, R_VB:R_VB + W_GRP]
    mu = jnp.mean(vb, axis=1, keepdims=True)
    vc_ = vb - mu
    var = jnp.mean(vc_ * vc_, axis=1, keepdims=True)
    vbn = vc_ * lax.rsqrt(var + LN_EPS) * glg_ref[...] + glb_ref[...]
    if emit_vbn:
        vbn_ref[...] = vbn
    for g in range(GB):
        gs = slice(g * LANES, (g + 1) * LANES)
        mixed = _dot(wsm_s[g], vbn[:, gs].astype(BF16)) + bs_ref[:, g:g + 1]
        ub = rest_ref[:, R_UB + g * LANES:R_UB + (g + 1) * LANES]
        gb = rest_ref[:, R_GB + g * LANES:R_GB + (g + 1) * LANES]
        out_ref[:, gs] = (ub * mixed * _silu(gb)).astype(out_ref.dtype)

    for h in range(HC):
        log_g = ret_log_g[h]
        q = rest_ref[:, R_QC + h * DKC:R_QC + (h + 1) * DKC].astype(BF16)
        k = rest_ref[:, R_KC + h * DKC:R_KC + (h + 1) * DKC] * (DKC ** -0.5)
        v = rest_ref[:, R_VC + h * DVC:R_VC + (h + 1) * DVC]
        scores = _dot_nt(q, k.astype(BF16)) * dec_s[h]
        o = _dot(scores.astype(BF16), v.astype(BF16))
        row_scale = jnp.exp(log_g * (tpos + 1.0))
        vw = (v * jnp.exp(log_g * (L - 1.0 - tpos))).astype(BF16)
        g_len = math.exp(log_g * L)

        def ret_body(j, rm, o_state, h=h, q=q, k=k, vw=vw, g_len=g_len):
            s0 = ret_ref[j, h]
            o_state = o_state + rows_of(rm, _dot(q, s0.astype(BF16)))
            ret_ref[j, h] = g_len * s0 + _dot_tn(rows_of(rm, k).astype(BF16), vw)
            return o_state

        o = o + over_segments(ret_body, jnp.zeros((R, DVC), F32)) * row_scale
        mu = jnp.mean(o, axis=1, keepdims=True)
        oc = o - mu
        var = jnp.mean(oc * oc, axis=1, keepdims=True)
        hs = slice(h * DVC, (h + 1) * DVC)
        oc = oc * lax.rsqrt(var + LN_EPS) * rgg_ref[:, hs] + rgb_ref[:, hs]
        gc = rest_ref[:, R_GC + h * DVC:R_GC + (h + 1) * DVC]
        out_ref[:, W_GRP + h * DVC:W_GRP + (h + 1) * DVC] = (oc * _silu(gc)).astype(out_ref.dtype)

    x_cur = rest_ref[:, R_XBC:R_XBC + CONV_DIM]
    xpad_s[:, 8:8 + L, :] = x_cur.reshape(G, L, CONV_DIM)
    conv = cb_ref[...] + cw_ref[CONV_W - 1:CONV_W, :] * x_cur
    for kk in range(1, CONV_W):
        shifted = xpad_s[:, 8 - kk:8 - kk + L, :].reshape(R, CONV_DIM)
        conv = conv + cw_ref[CONV_W - 1 - kk:CONV_W - kk, :] * shifted
    convn_ref[...] = xpad_s[:, L + 8 - (CONV_W - 1):L + 8, :]
    xpad_s[:, 0:8, :] = xpad_s[:, L:L + 8, :]
    conv = _silu(conv)
    xd = conv[:, :W_GRP]
    bd = conv[:, W_GRP:W_GRP + NG * NS]
    cd = conv[:, W_GRP + NG * NS:]

    zt = rest_ref[:, R_DT:R_DT + LANES] + dtb_ref[...]
    dt = jnp.maximum(zt, 0.0) + jnp.log1p(jnp.exp(-jnp.abs(zt)))
    dta = dt * (-jnp.exp(alog_ref[...]))
    cum = jnp.dot(mask.astype(F32), dta, precision=lax.Precision.HIGHEST, preferred_element_type=F32)
    cum_last = jnp.dot(same.astype(F32), dta, precision=lax.Precision.HIGHEST,
                       preferred_element_type=F32)
    cum_t = cum.T
    dt_t = dt.T
    ecum = jnp.exp(cum)
    wlast = jnp.exp(cum_last - cum) * dt
    dl_s[...] = jnp.exp(cum_last)

    yd_parts = []
    for grp in range(NG):
        bg = bd[:, grp * NS:(grp + 1) * NS].astype(BF16)
        cg = cd[:, grp * NS:(grp + 1) * NS].astype(BF16)
        cb = _dot_nt(cg, bg)
        for hh in range(HD // NG):
            h = grp * (HD // NG) + hh
            seg = cum[:, h:h + 1] - cum_t[h:h + 1, :]
            lmat = jnp.where(mask, jnp.exp(jnp.where(mask, seg, 0.0)), 0.0)
            scores = cb * lmat * dt_t[h:h + 1, :]
            x_h = xd[:, h * PD:(h + 1) * PD]
            y = _dot(scores.astype(BF16), x_h.astype(BF16))
            xw = x_h * wlast[:, h:h + 1]

            def ssd_body(j, rm, y_state, h=h, cg=cg, bg=bg, xw=xw):
                h0 = ssm_ref[j, h]
                y_state = y_state + rows_of(rm, _dot_nt(cg, h0.astype(BF16)))
                dec = dl_s[j * L:j * L + 1, h:h + 1]
                ssm_ref[j, h] = dec * h0 + _dot_tn(rows_of(rm, xw).astype(BF16), bg)
                return y_state

            y = y + over_segments(ssd_body, jnp.zeros((R, PD), F32)) * ecum[:, h:h + 1]
            yd_parts.append(y + dskip_ref[:, h * PD:(h + 1) * PD] * x_h)
    yd = jnp.concatenate(yd_parts, axis=1)
    z = yd * _silu(rest_ref[:, R_ZD:R_ZD + W_GRP])
    out_ref[:, 2 * W_GRP:3 * W_GRP] = (
        z * lax.rsqrt(jnp.mean(z * z, axis=1, keepdims=True) + LN_EPS) * sng_ref[...]
    ).astype(out_ref.dtype)


def _mixers(rest, ret0, ssm0, conv0, p, *, seg_len, n_chunks, emit_vbn):
    t = rest.shape[0]
    n_seq = ret0.shape[0]
    G = TILE_ROWS // seg_len
    n_grp = n_seq // G
    kern = functools.partial(_mixer_kernel, seg_len=seg_len, n_seg=G, emit_vbn=emit_vbn)

    def full(shape):
        nd = len(shape)
        return pl.BlockSpec(shape, lambda g, c, _nd=nd: (0,) * _nd)

    in_specs = [pl.BlockSpec((TILE_ROWS, REST_COLS), lambda g, c: (g * n_chunks + c, 0)),
                pl.BlockSpec((G, HC, DKC, DVC), lambda g, c: (g, 0, 0, 0)),
                pl.BlockSpec((G, HD, PD, NS), lambda g, c: (g, 0, 0, 0)),
                pl.BlockSpec((G, CONV_W - 1, CONV_DIM), lambda g, c: (g, 0, 0)),
                full((1, W_GRP)), full((1, W_GRP)), full((GB, TILE_ROWS, TILE_ROWS)),
                full((TILE_ROWS, GB)), full((1, W_GRP)), full((1, W_GRP)),
                full((CONV_W, CONV_DIM)), full((1, CONV_DIM)), full((1, LANES)), full((1, LANES)),
                full((1, W_GRP)), full((1, W_GRP))]
    out_specs = [pl.BlockSpec((TILE_ROWS, 3 * W_GRP), lambda g, c: (g * n_chunks + c, 0)),
                 pl.BlockSpec((G, HC, DKC, DVC), lambda g, c: (g, 0, 0, 0)),
                 pl.BlockSpec((G, HD, PD, NS), lambda g, c: (g, 0, 0, 0)),
                 pl.BlockSpec((G, CONV_W - 1, CONV_DIM), lambda g, c: (g, 0, 0))]
    out_shape = [jax.ShapeDtypeStruct((t, 3 * W_GRP), BF16),
                 jax.ShapeDtypeStruct((n_seq, HC, DKC, DVC), F32),
                 jax.ShapeDtypeStruct((n_seq, HD, PD, NS), F32),
                 jax.ShapeDtypeStruct((n_seq, CONV_W - 1, CONV_DIM), F32)]
    if emit_vbn:
        out_specs.append(pl.BlockSpec((TILE_ROWS, W_GRP), lambda g, c: (g * n_chunks + c, 0)))
        out_shape.append(jax.ShapeDtypeStruct((t, W_GRP), F32))
    return pl.pallas_call(
        kern,
        grid=(n_grp, n_chunks),
        in_specs=in_specs,
        out_specs=out_specs,
        out_shape=out_shape,
        scratch_shapes=[pltpu.VMEM((G, seg_len + 8, CONV_DIM), F32),
                        pltpu.VMEM((TILE_ROWS, LANES), F32),
                        pltpu.VMEM((GB, TILE_ROWS, TILE_ROWS), BF16),
                        pltpu.VMEM((HC, TILE_ROWS, TILE_ROWS), F32)],
        compiler_params=pltpu.CompilerParams(dimension_semantics=("arbitrary", "arbitrary"),
                                             vmem_limit_bytes=VMEM_LIMIT),
        name="mixers",
    )(rest, ret0, ssm0, conv0, p["glg"], p["glb"], p["ws"], p["bs"], p["rgg"], p["rgb"],
      p["cw"], p["cb"], p["dtb"], p["alog"], p["dskip"], p["sng"])


def _outproj_kernel(a_ref, m_ref, x_ref, w_ref, g_ref, b_ref, y_ref, *, alpha):
    hproj = _dot(a_ref[...].astype(BF16), w_ref[0:W_GRP, :])
    hproj = hproj + _dot(m_ref[...], w_ref[W_GRP:, :])
    z = alpha * x_ref[...] + hproj
    mu = jnp.mean(z, axis=1, keepdims=True)
    zc = z - mu
    var = jnp.mean(zc * zc, axis=1, keepdims=True)
    y_ref[...] = zc * lax.rsqrt(var + LN_EPS) * g_ref[...] + b_ref[...]


def _outproj(out_a, out_bcd, x2, w_out, layer, ln_g, ln_b, tm, alpha):
    t, d = x2.shape
    kern = functools.partial(_outproj_kernel, alpha=alpha)
    return pl.pallas_call(
        kern,
        grid=(t // tm,),
        in_specs=[pl.BlockSpec((tm, W_GRP), lambda i: (i, 0)),
                  pl.BlockSpec((tm, 3 * W_GRP), lambda i: (i, 0)),
                  pl.BlockSpec((tm, d), lambda i: (i, 0)),
                  pl.BlockSpec((None, 4 * W_GRP, d), lambda i: (layer, 0, 0)),
                  pl.BlockSpec((1, d), lambda i: (0, 0)),
                  pl.BlockSpec((1, d), lambda i: (0, 0))],
        out_specs=pl.BlockSpec((tm, d), lambda i: (i, 0)),
        out_shape=jax.ShapeDtypeStruct((t, d), F32),
        compiler_params=pltpu.CompilerParams(dimension_semantics=("arbitrary",),
                                             vmem_limit_bytes=VMEM_LIMIT),
        name="outproj",
    )(out_a, out_bcd, x2, w_out, ln_g, ln_b)


def _layer_params(l, seg_len, gmlp_ln_g, gmlp_ln_b, gmlp_ws, gmlp_bs, ret_gn_g, ret_gn_b, conv_w,
                  conv_b, dt_bias, a_log, d_skip, ssd_norm_g):
    G = TILE_ROWS // seg_len
    pad8 = lambda v: jnp.pad(v.astype(F32), (0, LANES - HD)).reshape(1, LANES)
    return {
        "glg": gmlp_ln_g[l].reshape(1, W_GRP), "glb": gmlp_ln_b[l].reshape(1, W_GRP),
        "ws": jnp.tile(gmlp_ws[l][:, :seg_len, :seg_len], (1, G, G)),
        "bs": jnp.tile(gmlp_bs[l][:, :seg_len].T, (G, 1)),
        "rgg": ret_gn_g[l].reshape(1, W_GRP), "rgb": ret_gn_b[l].reshape(1, W_GRP),
        "cw": conv_w[l], "cb": conv_b[l].reshape(1, CONV_DIM),
        "dtb": pad8(dt_bias[l]), "alog": pad8(a_log[l]),
        "dskip": jnp.repeat(d_skip[l].astype(F32), PD).reshape(1, W_GRP),
        "sng": ssd_norm_g[l].reshape(1, W_GRP),
    }


def kernel(x_prompt, x_sample, cache_k, cache_v, page_table, state_ret, state_ssm, state_conv, w_in, w_out, ln_g, ln_b, lam_q1, lam_k1, lam_q2, lam_k2, subln_g, gmlp_ln_g, gmlp_ln_b, gmlp_ws, gmlp_bs, ret_gn_g, ret_gn_b, conv_w, conv_b, dt_bias, a_log, d_skip, ssd_norm_g):
    depth = w_in.shape[0]
    batch, seq, d_model = x_prompt.shape
    dec_b, dec_t, _ = x_sample.shape
    n_pages = page_table.shape[1]
    in_cols = w_in.shape[2]
    alpha = (2.0 * depth) ** 0.25
    assert in_cols == QKV_COLS + R_DT + HD and seq % TILE_ROWS == 0 and TILE_ROWS % dec_t == 0
    assert (dec_b * dec_t) % TILE_ROWS == 0

    w_in_p = jnp.pad(w_in, ((0, 0), (0, 0), (0, QKV_COLS + REST_COLS - in_cols))).astype(BF16)
    w_out_b = w_out.astype(BF16)
    ck = cache_k.reshape(cache_k.shape[0], cache_k.shape[1], PAGE * HA, LANES)
    cv = cache_v.reshape(cache_v.shape[0], cache_v.shape[1], PAGE * HA, LANES)
    pt_flat = page_table.reshape(-1).astype(jnp.int32)
    zeros_ret = jnp.zeros((batch, HC, DKC, DVC), F32)
    zeros_ssm = jnp.zeros((batch, HD, PD, NS), F32)
    zeros_conv = jnp.zeros((batch, CONV_W - 1, CONV_DIM), F32)
    mix_args = (gmlp_ln_g, gmlp_ln_b, gmlp_ws, gmlp_bs, ret_gn_g, ret_gn_b, conv_w, conv_b, dt_bias,
                a_log, d_skip, ssd_norm_g)

    tp = batch * seq
    tsm = dec_b * dec_t
    tq = min(256, seq)
    npg = next(n for n in (16, 8, 4, 2, 1) if n_pages % n == 0)
    yp = x_prompt.reshape(tp, d_model)
    ys = x_sample.reshape(tsm, d_model)
    outs = [[] for _ in range(11)]
    for l in range(depth):
        lam_init = 0.8 - 0.6 * math.exp(-0.3 * l)
        lamv = jnp.stack([lam_q1[l], lam_k1[l], lam_q2[l], lam_k2[l]]).astype(F32)
        subln = subln_g[l].reshape(1, LANES)
        lng = ln_g[l].reshape(1, d_model)
        lnb = ln_b[l].reshape(1, d_model)

        pp = _layer_params(l, TILE_ROWS, *mix_args)
        q, k, v, kb, vb, rest = _inproj(yp, w_in_p, l, 512, with_bf16_kv=True)
        oa = _attn_prompt(q, kb, vb, rest, lamv, subln, batch=batch, seq=seq, tq=tq, lam_init=lam_init)
        obcd, rp, hp, cp = _mixers(rest, zeros_ret, zeros_ssm, zeros_conv, pp, seg_len=TILE_ROWS,
                                   n_chunks=seq // TILE_ROWS, emit_vbn=False)
        yp = _outproj(oa, obcd, yp, w_out_b, l, lng, lnb, 512, alpha)

        ps = _layer_params(l, dec_t, *mix_args)
        qs, ksn, vsn, rest_s = _inproj(ys, w_in_p, l, min(256, tsm), with_bf16_kv=False)
        oas = _attn_decode(pt_flat, qs, ksn, vsn, rest_s, lamv, subln, ck, cv, layer=l, dec_b=dec_b,
                           dec_t=dec_t, n_pages=n_pages, npg=npg, lam_init=lam_init)
        obcd_s, rs, hs, cs, gs = _mixers(rest_s, state_ret[l], state_ssm[l], state_conv[l], ps,
                                         seg_len=dec_t, n_chunks=1, emit_vbn=True)
        ys = _outproj(oas, obcd_s, ys, w_out_b, l, lng, lnb, min(256, tsm), alpha)

        for lst, val in zip(outs, (k.reshape(batch, seq, HA, 2 * DA), v.reshape(batch, seq, HA, 2 * DA),
                                   ksn.reshape(dec_b, dec_t, HA, 2 * DA),
                                   vsn.reshape(dec_b, dec_t, HA, 2 * DA),
                                   rp, rs, hp, hs, cp, cs, gs.reshape(dec_b, dec_t, W_GRP))):
            lst.append(val)
    stacked = [jnp.stack(o) for o in outs]
    return (yp.reshape(batch, seq, d_model), ys.reshape(dec_b, dec_t, d_model), *stacked)
```

```python
import functools
import math

import jax
import jax.numpy as jnp
from jax import lax
from jax.experimental import pallas as pl
from jax.experimental.pallas import tpu as pltpu

F32 = jnp.float32
BF16 = jnp.bfloat16

LANES = 128
TILE_ROWS = 128
HA = 4
DA = 64
GB = 4
HC = 4
DKC = 64
DVC = 128
HD = 8
PD = 64
NS = 128
NG = 2
CONV_W = 4
W_GRP = 512
CONV_DIM = W_GRP + 2 * NG * NS
LN_EPS = 1e-5
NEG_INF = -1e30
PAGE = 128
VMEM_LIMIT = 56 * 1024 * 1024
LOG2E = math.log2(math.e)
Q_SCALE = DA ** -0.5 * LOG2E

R_GA, R_UB, R_VB, R_GB = 0, 512, 1024, 1536
R_QC, R_KC, R_VC, R_GC = 2048, 2304, 2560, 3072
R_ZD, R_XBC, R_DT = 3584, 4096, 5120
REST_COLS = 5248
QKV_COLS = 3 * W_GRP
MIX_COL0 = R_UB
FUSE_ROWS = 2 * TILE_ROWS
N_MIX_PARAMS = 12


def _dot(a, b):
    return jnp.dot(a, b, preferred_element_type=F32)


def _dot_nt(a, b):
    return lax.dot_general(a, b, (((1,), (1,)), ((), ())), preferred_element_type=F32)


def _dot_tn(a, b):
    return lax.dot_general(a, b, (((0,), (0,)), ((), ())), preferred_element_type=F32)


def _silu(x):
    return x * (1.0 / (1.0 + jnp.exp(-x)))


def _alibi_slope2(h):
    return 2.0 ** (-8.0 * (h + 1) / HA) * LOG2E


def _inproj_kernel(x_ref, w_ref, *out_refs, with_bf16_kv):
    if with_bf16_kv:
        q_ref, k_ref, v_ref, kb_ref, vb_ref, r_ref = out_refs
    else:
        q_ref, k_ref, v_ref, r_ref = out_refs
    xb = x_ref[...].astype(BF16)
    q_ref[...] = (_dot(xb, w_ref[:, 0:W_GRP]) * Q_SCALE).astype(q_ref.dtype)
    k = _dot(xb, w_ref[:, W_GRP:2 * W_GRP])
    v = _dot(xb, w_ref[:, 2 * W_GRP:3 * W_GRP])
    for h in range(HA):
        k_ref[:, h, :] = k[:, h * LANES:(h + 1) * LANES]
        v_ref[:, h, :] = v[:, h * LANES:(h + 1) * LANES]
    if with_bf16_kv:
        kb_ref[...] = k.astype(BF16)
        vb_ref[...] = v.astype(BF16)
    r_ref[...] = _dot(xb, w_ref[:, QKV_COLS:])


def _inproj(x2, w_pad, layer, tm, *, ncols, with_bf16_kv):
    t, d = x2.shape
    row = lambda i: (i, 0)
    out_specs = [pl.BlockSpec((tm, W_GRP), row),
                 pl.BlockSpec((tm, HA, LANES), lambda i: (i, 0, 0)),
                 pl.BlockSpec((tm, HA, LANES), lambda i: (i, 0, 0))]
    out_shape = [jax.ShapeDtypeStruct((t, W_GRP), BF16 if with_bf16_kv else F32),
                 jax.ShapeDtypeStruct((t, HA, LANES), F32),
                 jax.ShapeDtypeStruct((t, HA, LANES), F32)]
    if with_bf16_kv:
        out_specs += [pl.BlockSpec((tm, W_GRP), row), pl.BlockSpec((tm, W_GRP), row)]
        out_shape += [jax.ShapeDtypeStruct((t, W_GRP), BF16), jax.ShapeDtypeStruct((t, W_GRP), BF16)]
    out_specs.append(pl.BlockSpec((tm, ncols - QKV_COLS), row))
    out_shape.append(jax.ShapeDtypeStruct((t, ncols - QKV_COLS), F32))
    return pl.pallas_call(
        functools.partial(_inproj_kernel, with_bf16_kv=with_bf16_kv),
        grid=(t // tm,),
        in_specs=[pl.BlockSpec((tm, d), row),
                  pl.BlockSpec((None, d, ncols), lambda i: (layer, 0, 0), pipeline_mode=pl.Buffered(1))],
        out_specs=out_specs,
        out_shape=out_shape,
        compiler_params=pltpu.CompilerParams(dimension_semantics=("arbitrary",),
                                             vmem_limit_bytes=VMEM_LIMIT),
        name="inproj",
    )(x2, w_pad)


def _split_maps(q):
    lane = lax.broadcasted_iota(jnp.int32, q.shape, 1)
    zero = jnp.zeros_like(q)
    q1 = jnp.where(lane < DA, q, zero)
    q2 = jnp.where(lane >= DA, q, zero)
    return jnp.concatenate([q1, q2], axis=0).astype(BF16)


def _lambda_value(lam_ref, lam_init):
    lv = lam_ref[...]
    a = jnp.sum(lv[0:1, :] * lv[1:2, :], axis=1, keepdims=True)
    b = jnp.sum(lv[2:3, :] * lv[3:4, :], axis=1, keepdims=True)
    return jnp.exp(a) - jnp.exp(b) + lam_init


def _finish_head(o, n, lam, lam_init, sg, ga):
    out = o[:n] - lam * o[n:]
    out = out * lax.rsqrt(jnp.mean(out * out, axis=1, keepdims=True) + LN_EPS) * sg * (1.0 - lam_init)
    return out * _silu(ga)


def _attn_prompt_kernel(lam_ref, q_ref, k_ref, v_ref, ga_ref, sg_ref, o_ref,
                        q2_s, s0_s, s1_s, p0_s, p1_s, m_s, a_s, l_s, acc_s, *, tq, nq, lam_init):
    h = pl.program_id(1)
    slope = (jnp.where(h == 0, 2.0 ** -2, jnp.where(h == 1, 2.0 ** -4,
                       jnp.where(h == 2, 2.0 ** -6, 2.0 ** -8))) * LOG2E).astype(F32)
    s_bufs = (s0_s, s1_s)
    p_bufs = (p0_s, p1_s)
    n_tiles = nq * (nq + 1) // 2
    for i in range(nq):
        q2_s[i] = _split_maps(q_ref[i * tq:(i + 1) * tq, :])
    m_s[...] = jnp.full(m_s.shape, NEG_INF, F32)
    l_s[...] = jnp.zeros(l_s.shape, F32)
    acc_s[...] = jnp.zeros(acc_s.shape, F32)
    col = lax.broadcasted_iota(jnp.int32, (1, tq), 1)

    def tile(ref, j):
        return ref[pl.ds(pl.multiple_of(j * tq, tq), tq), :]

    def qk(i, j, slot):
        s_bufs[slot][...] = _dot_nt(q2_s[i], tile(k_ref, j))

    def pv(i, j, slot):
        acc_s[i] = a_s[...] * acc_s[i] + _dot(p_bufs[slot][...], tile(v_ref, j))

    def softmax(i, j, slot, masked):
        bias = slope * (col + (j - i) * tq).astype(F32)
        if masked:
            r = lax.broadcasted_iota(jnp.int32, (2 * tq, LANES), 0) & (tq - 1)
            c = lax.broadcasted_iota(jnp.int32, (2 * tq, LANES), 1)
        sb = []
        for cb in range(tq // LANES):
            cols = slice(cb * LANES, (cb + 1) * LANES)
            x = s_bufs[slot][:, cols] + bias[:, cols]
            if masked:
                x = jnp.where(c + cb * LANES <= r, x, NEG_INF)
            sb.append(x)
        m_old = m_s[i]
        m_new = jnp.maximum(m_old, jnp.max(functools.reduce(jnp.maximum, sb), axis=1, keepdims=True))
        alpha = jnp.exp2(m_old - m_new)
        pb = [jnp.exp2(x - m_new) for x in sb]
        l_s[i] = alpha * l_s[i] + functools.reduce(jnp.add, pb)
        for cb in range(tq // LANES):
            p_bufs[slot][:, cb * LANES:(cb + 1) * LANES] = pb[cb].astype(BF16)
        m_s[i] = m_new
        a_s[...] = alpha

    def step(i, j, cur, masked):
        last_of_row = j == i
        qk(jnp.where(last_of_row, i + 1, i), jnp.where(last_of_row, 0, j + 1), 1 - cur)
        first_of_row = j == 0
        pv(jnp.where(first_of_row, i - 1, i), jnp.where(first_of_row, i - 1, j - 1), 1 - cur)
        softmax(i, j, cur, masked)

    qk(0, 0, 0)
    qk(1, 0, 1)
    softmax(0, 0, 0, True)

    def body(t, carry):
        i, j = carry
        for cur in (0, 1):
            for masked in (False, True):
                on_diagonal = (j == i) if masked else (j != i)
                pl.when(((t & 1) == cur) & on_diagonal)(functools.partial(step, i, j, cur, masked))
        last_of_row = j == i
        return jnp.where(last_of_row, i + 1, i), jnp.where(last_of_row, 0, j + 1)

    lax.fori_loop(1, n_tiles - 1, body, (jnp.int32(1), jnp.int32(0)))

    cur = (n_tiles - 1) & 1
    pv(nq - 1, nq - 2, 1 - cur)
    softmax(nq - 1, nq - 1, cur, True)
    pv(nq - 1, nq - 1, cur)

    lam = _lambda_value(lam_ref, lam_init)
    for i in range(nq):
        rows = slice(i * tq, (i + 1) * tq)
        o = acc_s[i] / jnp.sum(l_s[i], axis=1, keepdims=True)
        o_ref[rows, :] = _finish_head(o, tq, lam, lam_init, sg_ref[...], ga_ref[rows, :]).astype(o_ref.dtype)


def _attn_prompt(q, kb, vb, rest, lamv, subln, *, batch, seq, tq, lam_init):
    assert tq & (tq - 1) == 0 and tq % LANES == 0 and seq % tq == 0 and seq // tq >= 2
    t = batch * seq
    nq = seq // tq
    kern = functools.partial(_attn_prompt_kernel, tq=tq, nq=nq, lam_init=lam_init)
    head_block = pl.BlockSpec((seq, LANES), lambda b, h: (b, h))
    return pl.pallas_call(
        kern,
        grid=(batch, HA),
        in_specs=[pl.BlockSpec((4, DA), lambda b, h: (0, 0)),
                  head_block, head_block, head_block, head_block,
                  pl.BlockSpec((1, LANES), lambda b, h: (0, 0))],
        out_specs=head_block,
        out_shape=jax.ShapeDtypeStruct((t, W_GRP), BF16),
        scratch_shapes=[pltpu.VMEM((nq, 2 * tq, LANES), BF16),
                        pltpu.VMEM((2 * tq, tq), F32), pltpu.VMEM((2 * tq, tq), F32),
                        pltpu.VMEM((2 * tq, tq), BF16), pltpu.VMEM((2 * tq, tq), BF16),
                        pltpu.VMEM((nq, 2 * tq, LANES), F32), pltpu.VMEM((2 * tq, LANES), F32),
                        pltpu.VMEM((nq, 2 * tq, LANES), F32), pltpu.VMEM((nq, 2 * tq, LANES), F32)],
        compiler_params=pltpu.CompilerParams(
            dimension_semantics=("arbitrary", "arbitrary"), vmem_limit_bytes=VMEM_LIMIT),
        name="attn_prompt",
    )(lamv, q, kb, vb, rest, subln)


def _attn_decode_kernel(pt_ref, lam_ref, q_ref, kn_ref, vn_ref, ga_ref, sg_ref, *refs,
                        npg, past_len, lam_init):
    del pt_ref
    k_refs = refs[:npg]
    v_refs = refs[npg:2 * npg]
    o_ref = refs[2 * npg]
    m_s, l_s, acc_s = refs[2 * npg + 1:]
    j = pl.program_id(1)
    nj = pl.num_programs(1)
    nq = q_ref.shape[0]
    rows = 2 * nq
    blk = npg * PAGE

    @pl.when(j == 0)
    def _():
        m_s[...] = jnp.full(m_s.shape, NEG_INF, F32)
        l_s[...] = jnp.zeros(l_s.shape, F32)
        acc_s[...] = jnp.zeros(acc_s.shape, F32)

    q2 = [_split_maps(q_ref[:, h * LANES:(h + 1) * LANES]) for h in range(HA)]
    rowi = lax.broadcasted_iota(jnp.int32, (HA * rows, 1), 0)
    slope_col = jnp.full((HA * rows, 1), _alibi_slope2(HA - 1), F32)
    for h in range(HA - 2, -1, -1):
        slope_col = jnp.where(rowi < (h + 1) * rows, _alibi_slope2(h), slope_col)

    def update(s, v_of_head):
        m_old = m_s[...]
        m_new = jnp.maximum(m_old, jnp.max(s, axis=1, keepdims=True))
        alpha = jnp.exp2(m_old - m_new)
        p = jnp.exp2(s - m_new)
        l_new = alpha * l_s[...] + jnp.sum(p, axis=1, keepdims=True)
        p = p.astype(BF16)
        pv = jnp.concatenate([_dot(p[h * rows:(h + 1) * rows], v_of_head(h)) for h in range(HA)], axis=0)
        return m_new, l_new, alpha * acc_s[...] + pv

    def k_past(h):
        return jnp.concatenate([r[pl.ds(h, PAGE, stride=HA), :] for r in k_refs], axis=0).astype(BF16)

    def v_past(h):
        return jnp.concatenate([r[pl.ds(h, PAGE, stride=HA), :] for r in v_refs], axis=0).astype(BF16)

    col = lax.broadcasted_iota(jnp.int32, (1, blk), 1)
    s = jnp.concatenate([_dot_nt(q2[h], k_past(h)) for h in range(HA)], axis=0)
    s = s + slope_col * (col + (j * blk - past_len)).astype(F32)
    m_new, l_new, acc_new = update(s, v_past)
    m_s[...] = m_new
    l_s[...] = l_new
    acc_s[...] = acc_new

    @pl.when(j == nj - 1)
    def _():
        lam = _lambda_value(lam_ref, lam_init)
        zpad = jnp.zeros((LANES - nq, LANES), F32)
        kn = lambda h: jnp.concatenate([kn_ref[:, h, :], zpad], axis=0).astype(BF16)
        vn = lambda h: jnp.concatenate([vn_ref[:, h, :], zpad], axis=0).astype(BF16)
        r = lax.broadcasted_iota(jnp.int32, (HA * rows, LANES), 0) & (nq - 1)
        c = lax.broadcasted_iota(jnp.int32, (HA * rows, LANES), 1)
        sn = jnp.concatenate([_dot_nt(q2[h], kn(h)) for h in range(HA)], axis=0)
        sn = jnp.where(c <= r, sn + slope_col * c.astype(F32), NEG_INF)
        _, l_fin, acc_fin = update(sn, vn)
        o = acc_fin / l_fin
        for h in range(HA):
            hs = slice(h * LANES, (h + 1) * LANES)
            o_ref[:, hs] = _finish_head(o[h * rows:(h + 1) * rows], nq, lam, lam_init, sg_ref[...],
                                        ga_ref[:, hs])


def _attn_decode(pt_flat, q, kn, vn, rest, lamv, subln, cache_k, cache_v, *, layer, dec_b, dec_t,
                 n_pages, npg, lam_init):
    assert dec_t & (dec_t - 1) == 0
    past_len = n_pages * PAGE
    kern = functools.partial(_attn_decode_kernel, npg=npg, past_len=past_len, lam_init=lam_init)

    def page_spec(i):
        return pl.BlockSpec((None, None, PAGE * HA, LANES),
                            lambda b, j, pt: (layer, pt[b * n_pages + j * npg + i], 0, 0))

    row_spec = pl.BlockSpec((dec_t, W_GRP), lambda b, j, pt: (b, 0))
    new_spec = pl.BlockSpec((dec_t, HA, LANES), lambda b, j, pt: (b, 0, 0))
    grid_spec = pltpu.PrefetchScalarGridSpec(
        num_scalar_prefetch=1,
        grid=(dec_b, n_pages // npg),
        in_specs=[pl.BlockSpec((4, DA), lambda b, j, pt: (0, 0)),
                  row_spec, new_spec, new_spec, row_spec,
                  pl.BlockSpec((1, LANES), lambda b, j, pt: (0, 0))]
                 + [page_spec(i) for i in range(npg)] + [page_spec(i) for i in range(npg)],
        out_specs=row_spec,
        scratch_shapes=[pltpu.VMEM((HA * 2 * dec_t, 1), F32),
                        pltpu.VMEM((HA * 2 * dec_t, 1), F32),
                        pltpu.VMEM((HA * 2 * dec_t, LANES), F32)],
    )
    return pl.pallas_call(
        kern,
        grid_spec=grid_spec,
        out_shape=jax.ShapeDtypeStruct((dec_b * dec_t, W_GRP), F32),
        compiler_params=pltpu.CompilerParams(dimension_semantics=("arbitrary", "arbitrary"),
                                             vmem_limit_bytes=VMEM_LIMIT),
        name="attn_decode",
    )(pt_flat, lamv, q, kn, vn, rest, subln, *([cache_k] * npg), *([cache_v] * npg))


def _mixer_tile(rest_ref, is_first, ret0_ref, ssm0_ref, conv0_ref, glg_ref, glb_ref, ws_ref, bs_ref,
                rgg_ref, rgb_ref, cw_ref, cb_ref, dtb_ref, alog_ref, dskip_ref, sng_ref,
                out_ref, ret_ref, ssm_ref, convn_ref, vbn_ref, xpad_s, dl_s, wsm_s, dec_s,
                *, seg_len, n_seg, col0, after_init=None):
    def cols(start, width):
        return rest_ref[:, start - col0:start - col0 + width]

    L, G = seg_len, n_seg
    R = TILE_ROWS
    lg = int(math.log2(L))
    ret_log_g = [math.log(1.0 - 2.0 ** (-5.0 - h)) for h in range(HC)]

    tt = lax.broadcasted_iota(jnp.int32, (R, R), 0)
    ss = lax.broadcasted_iota(jnp.int32, (R, R), 1)
    same = (tt >> lg) == (ss >> lg)
    mask = same & (ss <= tt)

    def start_of_sequence():
        ret_ref[...] = ret0_ref[...]
        ssm_ref[...] = ssm0_ref[...]
        xpad_s[:, 8 - (CONV_W - 1):8, :] = conv0_ref[...]
        diff = jnp.maximum(tt - ss, 0).astype(F32)
        for g in range(GB):
            wsm_s[g] = jnp.where(mask, ws_ref[g], 0.0).astype(BF16)
        for h in range(HC):
            dec_s[h] = jnp.where(mask, jnp.exp(ret_log_g[h] * diff), 0.0)

    if is_first is not None:
        pl.when(is_first)(start_of_sequence)
    if after_init is not None:
        after_init()

    trow = lax.broadcasted_iota(jnp.int32, (R, 1), 0)
    tpos = (trow & (L - 1)).astype(F32)
    rseg = trow >> lg

    def over_segments(body, init):
        carry = init
        for j in range(G):
            carry = body(j, None if G == 1 else rseg == j, carry)
        return carry

    def rows_of(rm, x):
        return x if rm is None else jnp.where(rm, x, 0.0)

    vb = cols(R_VB, W_GRP)
    mu = jnp.mean(vb, axis=1, keepdims=True)
    vc_ = vb - mu
    var = jnp.mean(vc_ * vc_, axis=1, keepdims=True)
    vbn = vc_ * lax.rsqrt(var + LN_EPS) * glg_ref[...] + glb_ref[...]
    if vbn_ref is not None:
        vbn_ref[...] = vbn
    for g in range(GB):
        gs = slice(g * LANES, (g + 1) * LANES)
        mixed = _dot(wsm_s[g], vbn[:, gs].astype(BF16)) + bs_ref[:, g:g + 1]
        ub = cols(R_UB + g * LANES, LANES)
        gb = cols(R_GB + g * LANES, LANES)
        out_ref[:, gs] = (ub * mixed * _silu(gb)).astype(out_ref.dtype)

    for h in range(HC):
        log_g = ret_log_g[h]
        q = cols(R_QC + h * DKC, DKC).astype(BF16)
        k = cols(R_KC + h * DKC, DKC) * (DKC ** -0.5)
        v = cols(R_VC + h * DVC, DVC)
        scores = _dot_nt(q, k.astype(BF16)) * dec_s[h]
        o = _dot(scores.astype(BF16), v.astype(BF16))
        row_scale = jnp.exp(log_g * (tpos + 1.0))
        vw = (v * jnp.exp(log_g * (L - 1.0 - tpos))).astype(BF16)
        g_len = math.exp(log_g * L)

        def ret_body(j, rm, o_state, h=h, q=q, k=k, vw=vw, g_len=g_len):
            s0 = ret_ref[j, h]
            o_state = o_state + rows_of(rm, _dot(q, s0.astype(BF16)))
            ret_ref[j, h] = g_len * s0 + _dot_tn(rows_of(rm, k).astype(BF16), vw)
            return o_state

        o = o + over_segments(ret_body, jnp.zeros((R, DVC), F32)) * row_scale
        mu = jnp.mean(o, axis=1, keepdims=True)
        oc = o - mu
        var = jnp.mean(oc * oc, axis=1, keepdims=True)
        hs = slice(h * DVC, (h + 1) * DVC)
        oc = oc * lax.rsqrt(var + LN_EPS) * rgg_ref[:, hs] + rgb_ref[:, hs]
        gc = cols(R_GC + h * DVC, DVC)
        out_ref[:, W_GRP + h * DVC:W_GRP + (h + 1) * DVC] = (oc * _silu(gc)).astype(out_ref.dtype)

    x_cur = cols(R_XBC, CONV_DIM)
    xpad_s[:, 8:8 + L, :] = x_cur.reshape(G, L, CONV_DIM)
    conv = cb_ref[...] + cw_ref[CONV_W - 1:CONV_W, :] * x_cur
    for kk in range(1, CONV_W):
        shifted = xpad_s[:, 8 - kk:8 - kk + L, :].reshape(R, CONV_DIM)
        conv = conv + cw_ref[CONV_W - 1 - kk:CONV_W - kk, :] * shifted
    convn_ref[...] = xpad_s[:, L + 8 - (CONV_W - 1):L + 8, :]
    xpad_s[:, 0:8, :] = xpad_s[:, L:L + 8, :]
    conv = _silu(conv)
    xd = conv[:, :W_GRP]
    bd = conv[:, W_GRP:W_GRP + NG * NS]
    cd = conv[:, W_GRP + NG * NS:]

    zt = cols(R_DT, LANES) + dtb_ref[...]
    dt = jnp.maximum(zt, 0.0) + jnp.log1p(jnp.exp(-jnp.abs(zt)))
    dta = dt * (-jnp.exp(alog_ref[...]))
    cum = jnp.dot(mask.astype(F32), dta, precision=lax.Precision.HIGHEST, preferred_element_type=F32)
    cum_last = jnp.dot(same.astype(F32), dta, precision=lax.Precision.HIGHEST,
                       preferred_element_type=F32)
    cum_t = cum.T
    dt_t = dt.T
    ecum = jnp.exp(cum)
    wlast = jnp.exp(cum_last - cum) * dt
    dl_s[...] = jnp.exp(cum_last)

    yd_parts = []
    for grp in range(NG):
        bg = bd[:, grp * NS:(grp + 1) * NS].astype(BF16)
        cg = cd[:, grp * NS:(grp + 1) * NS].astype(BF16)
        cb = _dot_nt(cg, bg)
        for hh in range(HD // NG):
            h = grp * (HD // NG) + hh
            seg = cum[:, h:h + 1] - cum_t[h:h + 1, :]
            lmat = jnp.where(mask, jnp.exp(jnp.where(mask, seg, 0.0)), 0.0)
            scores = cb * lmat * dt_t[h:h + 1, :]
            x_h = xd[:, h * PD:(h + 1) * PD]
            y = _dot(scores.astype(BF16), x_h.astype(BF16))
            xw = x_h * wlast[:, h:h + 1]

            def ssd_body(j, rm, y_state, h=h, cg=cg, bg=bg, xw=xw):
                h0 = ssm_ref[j, h]
                y_state = y_state + rows_of(rm, _dot_nt(cg, h0.astype(BF16)))
                dec = dl_s[j * L:j * L + 1, h:h + 1]
                ssm_ref[j, h] = dec * h0 + _dot_tn(rows_of(rm, xw).astype(BF16), bg)
                return y_state

            y = y + over_segments(ssd_body, jnp.zeros((R, PD), F32)) * ecum[:, h:h + 1]
            yd_parts.append(y + dskip_ref[:, h * PD:(h + 1) * PD] * x_h)
    yd = jnp.concatenate(yd_parts, axis=1)
    z = yd * _silu(cols(R_ZD, W_GRP))
    out_ref[:, 2 * W_GRP:3 * W_GRP] = (
        z * lax.rsqrt(jnp.mean(z * z, axis=1, keepdims=True) + LN_EPS) * sng_ref[...]
    ).astype(out_ref.dtype)


def _mixer_kernel(rest_ref, ret0_ref, ssm0_ref, conv0_ref, *refs, seg_len, n_seg, emit_vbn):
    params = refs[:N_MIX_PARAMS]
    n_out = 5 if emit_vbn else 4
    outs = refs[N_MIX_PARAMS:N_MIX_PARAMS + n_out]
    scratch = refs[N_MIX_PARAMS + n_out:]
    out_ref, ret_ref, ssm_ref, convn_ref = outs[:4]
    _mixer_tile(rest_ref, pl.program_id(1) == 0, ret0_ref, ssm0_ref, conv0_ref, *params,
                out_ref, ret_ref, ssm_ref, convn_ref, outs[4] if emit_vbn else None, *scratch,
                seg_len=seg_len, n_seg=n_seg, col0=0)


def _fused_mixer_kernel(x_ref, w_ref, ret0_ref, ssm0_ref, conv0_ref, *refs, chunks_per_seq):
    params = refs[:N_MIX_PARAMS]
    out_ref, ret_ref, ssm_ref, convn_ref = refs[N_MIX_PARAMS:N_MIX_PARAMS + 4]
    rest0_s, rest1_s, xpad_s, dl_s, wsm_s, dec_s = refs[N_MIX_PARAMS + 4:]
    s = pl.program_id(0)
    tiles = FUSE_ROWS // TILE_ROWS

    @pl.when(s == 0)
    def _():
        rest1_s[...] = jnp.zeros(rest1_s.shape, F32)

    def run(cur):
        bufs = (rest0_s, rest1_s)

        def project():
            bufs[cur][...] = _dot(x_ref[...].astype(BF16), w_ref[...])

        for k in range(tiles):
            chunk0 = (s - 1) * tiles
            is_first = ((chunk0 & (chunks_per_seq - 1)) == 0) | (s == 0) if k == 0 else None
            rows = pl.ds(k * TILE_ROWS, TILE_ROWS)
            _mixer_tile(bufs[1 - cur].at[rows], is_first, ret0_ref, ssm0_ref, conv0_ref, *params,
                        out_ref.at[rows], ret_ref, ssm_ref, convn_ref, None,
                        xpad_s, dl_s, wsm_s, dec_s, seg_len=TILE_ROWS, n_seg=1, col0=MIX_COL0,
                        after_init=project if k == 0 else None)

    pl.when((s & 1) == 0)(lambda: run(0))
    pl.when((s & 1) == 1)(lambda: run(1))


def _mixer_param_specs(index_map_zero):
    shapes = [(1, W_GRP), (1, W_GRP), (GB, TILE_ROWS, TILE_ROWS), (TILE_ROWS, GB), (1, W_GRP),
              (1, W_GRP), (CONV_W, CONV_DIM), (1, CONV_DIM), (1, LANES), (1, LANES), (1, W_GRP),
              (1, W_GRP)]
    assert len(shapes) == N_MIX_PARAMS
    return [pl.BlockSpec(s, index_map_zero(len(s))) for s in shapes]


def _mixer_param_args(p):
    return (p["glg"], p["glb"], p["ws"], p["bs"], p["rgg"], p["rgb"], p["cw"], p["cb"], p["dtb"],
            p["alog"], p["dskip"], p["sng"])


def _fused_mixers(x2, w_mix, layer, ret0, ssm0, conv0, p, *, seq):
    t, d = x2.shape
    ncols = w_mix.shape[2]
    n = t // FUSE_ROWS
    steps_per_seq = seq // FUSE_ROWS
    chunks_per_seq = seq // TILE_ROWS
    assert chunks_per_seq & (chunks_per_seq - 1) == 0 and seq % FUSE_ROWS == 0
    n_seq = ret0.shape[0]
    seq_of = lambda s: jnp.maximum(s - 1, 0) // steps_per_seq
    in_specs = [pl.BlockSpec((FUSE_ROWS, d), lambda s: (jnp.minimum(s, n - 1), 0)),
                pl.BlockSpec((None, d, ncols), lambda s: (layer, 0, 0), pipeline_mode=pl.Buffered(1)),
                pl.BlockSpec((1, HC, DKC, DVC), lambda s: (seq_of(s), 0, 0, 0)),
                pl.BlockSpec((1, HD, PD, NS), lambda s: (seq_of(s), 0, 0, 0)),
                pl.BlockSpec((1, CONV_W - 1, CONV_DIM), lambda s: (seq_of(s), 0, 0))]
    in_specs += _mixer_param_specs(lambda nd: (lambda s, _nd=nd: (0,) * _nd))
    out_specs = [pl.BlockSpec((FUSE_ROWS, 3 * W_GRP), lambda s: (jnp.maximum(s - 1, 0), 0)),
                 pl.BlockSpec((1, HC, DKC, DVC), lambda s: (seq_of(s), 0, 0, 0)),
                 pl.BlockSpec((1, HD, PD, NS), lambda s: (seq_of(s), 0, 0, 0)),
                 pl.BlockSpec((1, CONV_W - 1, CONV_DIM), lambda s: (seq_of(s), 0, 0))]
    out_shape = [jax.ShapeDtypeStruct((t, 3 * W_GRP), BF16),
                 jax.ShapeDtypeStruct((n_seq, HC, DKC, DVC), F32),
                 jax.ShapeDtypeStruct((n_seq, HD, PD, NS), F32),
                 jax.ShapeDtypeStruct((n_seq, CONV_W - 1, CONV_DIM), F32)]
    return pl.pallas_call(
        functools.partial(_fused_mixer_kernel, chunks_per_seq=chunks_per_seq),
        grid=(n + 1,),
        in_specs=in_specs,
        out_specs=out_specs,
        out_shape=out_shape,
        scratch_shapes=[pltpu.VMEM((FUSE_ROWS, ncols), F32), pltpu.VMEM((FUSE_ROWS, ncols), F32),
                        pltpu.VMEM((1, TILE_ROWS + 8, CONV_DIM), F32),
                        pltpu.VMEM((TILE_ROWS, LANES), F32),
                        pltpu.VMEM((GB, TILE_ROWS, TILE_ROWS), BF16),
                        pltpu.VMEM((HC, TILE_ROWS, TILE_ROWS), F32)],
        compiler_params=pltpu.CompilerParams(dimension_semantics=("arbitrary",),
                                             vmem_limit_bytes=VMEM_LIMIT),
        name="fused_mixers",
    )(x2, w_mix, ret0, ssm0, conv0, *_mixer_param_args(p))


def _mixers(rest, ret0, ssm0, conv0, p, *, seg_len, n_chunks, emit_vbn):
    t = rest.shape[0]
    n_seq = ret0.shape[0]
    G = TILE_ROWS // seg_len
    n_grp = n_seq // G
    kern = functools.partial(_mixer_kernel, seg_len=seg_len, n_seg=G, emit_vbn=emit_vbn)

    in_specs = [pl.BlockSpec((TILE_ROWS, REST_COLS), lambda g, c: (g * n_chunks + c, 0)),
                pl.BlockSpec((G, HC, DKC, DVC), lambda g, c: (g, 0, 0, 0)),
                pl.BlockSpec((G, HD, PD, NS), lambda g, c: (g, 0, 0, 0)),
                pl.BlockSpec((G, CONV_W - 1, CONV_DIM), lambda g, c: (g, 0, 0))]
    in_specs += _mixer_param_specs(lambda nd: (lambda g, c, _nd=nd: (0,) * _nd))
    out_specs = [pl.BlockSpec((TILE_ROWS, 3 * W_GRP), lambda g, c: (g * n_chunks + c, 0)),
                 pl.BlockSpec((G, HC, DKC, DVC), lambda g, c: (g, 0, 0, 0)),
                 pl.BlockSpec((G, HD, PD, NS), lambda g, c: (g, 0, 0, 0)),
                 pl.BlockSpec((G, CONV_W - 1, CONV_DIM), lambda g, c: (g, 0, 0))]
    out_shape = [jax.ShapeDtypeStruct((t, 3 * W_GRP), BF16),
                 jax.ShapeDtypeStruct((n_seq, HC, DKC, DVC), F32),
                 jax.ShapeDtypeStruct((n_seq, HD, PD, NS), F32),
                 jax.ShapeDtypeStruct((n_seq, CONV_W - 1, CONV_DIM), F32)]
    if emit_vbn:
        out_specs.append(pl.BlockSpec((TILE_ROWS, W_GRP), lambda g, c: (g * n_chunks + c, 0)))
        out_shape.append(jax.ShapeDtypeStruct((t, W_GRP), F32))
    return pl.pallas_call(
        kern,
        grid=(n_grp, n_chunks),
        in_specs=in_specs,
        out_specs=out_specs,
        out_shape=out_shape,
        scratch_shapes=[pltpu.VMEM((G, seg_len + 8, CONV_DIM), F32),
                        pltpu.VMEM((TILE_ROWS, LANES), F32),
                        pltpu.VMEM((GB, TILE_ROWS, TILE_ROWS), BF16),
                        pltpu.VMEM((HC, TILE_ROWS, TILE_ROWS), F32)],
        compiler_params=pltpu.CompilerParams(dimension_semantics=("arbitrary", "arbitrary"),
                                             vmem_limit_bytes=VMEM_LIMIT),
        name="mixers",
    )(rest, ret0, ssm0, conv0, *_mixer_param_args(p))


def _outproj_kernel(a_ref, m_ref, x_ref, w_ref, g_ref, b_ref, y_ref, *, alpha):
    hproj = _dot(a_ref[...].astype(BF16), w_ref[0:W_GRP, :])
    hproj = hproj + _dot(m_ref[...], w_ref[W_GRP:, :])
    z = alpha * x_ref[...] + hproj
    mu = jnp.mean(z, axis=1, keepdims=True)
    zc = z - mu
    var = jnp.mean(zc * zc, axis=1, keepdims=True)
    y_ref[...] = zc * lax.rsqrt(var + LN_EPS) * g_ref[...] + b_ref[...]


def _outproj(out_a, out_bcd, x2, w_out, layer, ln_g, ln_b, tm, alpha):
    t, d = x2.shape
    kern = functools.partial(_outproj_kernel, alpha=alpha)
    return pl.pallas_call(
        kern,
        grid=(t // tm,),
        in_specs=[pl.BlockSpec((tm, W_GRP), lambda i: (i, 0)),
                  pl.BlockSpec((tm, 3 * W_GRP), lambda i: (i, 0)),
                  pl.BlockSpec((tm, d), lambda i: (i, 0)),
                  pl.BlockSpec((None, 4 * W_GRP, d), lambda i: (layer, 0, 0)),
                  pl.BlockSpec((1, d), lambda i: (0, 0)),
                  pl.BlockSpec((1, d), lambda i: (0, 0))],
        out_specs=pl.BlockSpec((tm, d), lambda i: (i, 0)),
        out_shape=jax.ShapeDtypeStruct((t, d), F32),
        compiler_params=pltpu.CompilerParams(dimension_semantics=("arbitrary",),
                                             vmem_limit_bytes=VMEM_LIMIT),
        name="outproj",
    )(out_a, out_bcd, x2, w_out, ln_g, ln_b)


def _layer_params(l, seg_len, gmlp_ln_g, gmlp_ln_b, gmlp_ws, gmlp_bs, ret_gn_g, ret_gn_b, conv_w,
                  conv_b, dt_bias, a_log, d_skip, ssd_norm_g):
    G = TILE_ROWS // seg_len
    pad8 = lambda v: jnp.pad(v.astype(F32), (0, LANES - HD)).reshape(1, LANES)
    return {
        "glg": gmlp_ln_g[l].reshape(1, W_GRP), "glb": gmlp_ln_b[l].reshape(1, W_GRP),
        "ws": jnp.tile(gmlp_ws[l][:, :seg_len, :seg_len], (1, G, G)),
        "bs": jnp.tile(gmlp_bs[l][:, :seg_len].T, (G, 1)),
        "rgg": ret_gn_g[l].reshape(1, W_GRP), "rgb": ret_gn_b[l].reshape(1, W_GRP),
        "cw": conv_w[l], "cb": conv_b[l].reshape(1, CONV_DIM),
        "dtb": pad8(dt_bias[l]), "alog": pad8(a_log[l]),
        "dskip": jnp.repeat(d_skip[l].astype(F32), PD).reshape(1, W_GRP),
        "sng": ssd_norm_g[l].reshape(1, W_GRP),
    }


def kernel(x_prompt, x_sample, cache_k, cache_v, page_table, state_ret, state_ssm, state_conv, w_in, w_out, ln_g, ln_b, lam_q1, lam_k1, lam_q2, lam_k2, subln_g, gmlp_ln_g, gmlp_ln_b, gmlp_ws, gmlp_bs, ret_gn_g, ret_gn_b, conv_w, conv_b, dt_bias, a_log, d_skip, ssd_norm_g):
    depth = w_in.shape[0]
    batch, seq, d_model = x_prompt.shape
    dec_b, dec_t, _ = x_sample.shape
    n_pages = page_table.shape[1]
    in_cols = w_in.shape[2]
    alpha = (2.0 * depth) ** 0.25
    assert in_cols == QKV_COLS + R_DT + HD and seq % TILE_ROWS == 0 and TILE_ROWS % dec_t == 0
    assert (dec_b * dec_t) % TILE_ROWS == 0

    w_in_p = jnp.pad(w_in, ((0, 0), (0, 0), (0, QKV_COLS + REST_COLS - in_cols))).astype(BF16)
    w_mix = w_in_p[:, :, QKV_COLS + MIX_COL0:]
    w_out_b = w_out.astype(BF16)
    ck = cache_k.reshape(cache_k.shape[0], cache_k.shape[1], PAGE * HA, LANES)
    cv = cache_v.reshape(cache_v.shape[0], cache_v.shape[1], PAGE * HA, LANES)
    pt_flat = page_table.reshape(-1).astype(jnp.int32)
    zeros_ret = jnp.zeros((batch, HC, DKC, DVC), F32)
    zeros_ssm = jnp.zeros((batch, HD, PD, NS), F32)
    zeros_conv = jnp.zeros((batch, CONV_W - 1, CONV_DIM), F32)
    mix_args = (gmlp_ln_g, gmlp_ln_b, gmlp_ws, gmlp_bs, ret_gn_g, ret_gn_b, conv_w, conv_b, dt_bias,
                a_log, d_skip, ssd_norm_g)

    tp = batch * seq
    tsm = dec_b * dec_t
    tq = min(256, seq)
    npg = next(n for n in (16, 8, 4, 2, 1) if n_pages % n == 0)
    yp = x_prompt.reshape(tp, d_model)
    ys = x_sample.reshape(tsm, d_model)
    outs = [[] for _ in range(11)]
    for l in range(depth):
        lam_init = 0.8 - 0.6 * math.exp(-0.3 * l)
        lamv = jnp.stack([lam_q1[l], lam_k1[l], lam_q2[l], lam_k2[l]]).astype(F32)
        subln = subln_g[l].reshape(1, LANES)
        lng = ln_g[l].reshape(1, d_model)
        lnb = ln_b[l].reshape(1, d_model)

        pp = _layer_params(l, TILE_ROWS, *mix_args)
        q, k, v, kb, vb, ga = _inproj(yp, w_in_p, l, 512, ncols=QKV_COLS + MIX_COL0, with_bf16_kv=True)
        oa = _attn_prompt(q, kb, vb, ga, lamv, subln, batch=batch, seq=seq, tq=tq, lam_init=lam_init)
        obcd, rp, hp, cp = _fused_mixers(yp, w_mix, l, zeros_ret, zeros_ssm, zeros_conv, pp, seq=seq)
        yp = _outproj(oa, obcd, yp, w_out_b, l, lng, lnb, 512, alpha)

        ps = _layer_params(l, dec_t, *mix_args)
        qs, ksn, vsn, rest_s = _inproj(ys, w_in_p, l, min(256, tsm), ncols=QKV_COLS + REST_COLS,
                                       with_bf16_kv=False)
        oas = _attn_decode(pt_flat, qs, ksn, vsn, rest_s, lamv, subln, ck, cv, layer=l, dec_b=dec_b,
                           dec_t=dec_t, n_pages=n_pages, npg=npg, lam_init=lam_init)
        obcd_s, rs, hs, cs, gs = _mixers(rest_s, state_ret[l], state_ssm[l], state_conv[l], ps,
                                         seg_len=dec_t, n_chunks=1, emit_vbn=True)
        ys = _outproj(oas, obcd_s, ys, w_out_b, l, lng, lnb, min(256, tsm), alpha)

        for lst, val in zip(outs, (k.reshape(batch, seq, HA, 2 * DA), v.reshape(batch, seq, HA, 2 * DA),
                                   ksn.reshape(dec_b, dec_t, HA, 2 * DA),
                                   vsn.reshape(dec_b, dec_t, HA, 2 * DA),
                                   rp, rs, hp, hs, cp, cs, gs.reshape(dec_b, dec_t, W_GRP))):
            lst.append(val)
    stacked = [jnp.stack(o) for o in outs]
    return (yp.reshape(batch, seq, d_model), ys.reshape(dec_b, dec_t, d_model), *stacked)
```

```python
import functools
import math

import jax
import jax.numpy as jnp
from jax import lax
from jax.experimental import pallas as pl
from jax.experimental.pallas import tpu as pltpu

F32 = jnp.float32
BF16 = jnp.bfloat16

LANES = 128
TILE_ROWS = 128
HA = 4
DA = 64
GB = 4
HC = 4
DKC = 64
DVC = 128
HD = 8
PD = 64
NS = 128
NG = 2
CONV_W = 4
W_GRP = 512
CONV_DIM = W_GRP + 2 * NG * NS
LN_EPS = 1e-5
NEG_INF = -1e30
PAGE = 128
VMEM_LIMIT = 56 * 1024 * 1024
LOG2E = math.log2(math.e)
Q_SCALE = DA ** -0.5 * LOG2E

R_GA, R_UB, R_VB, R_GB = 0, 512, 1024, 1536
R_QC, R_KC, R_VC, R_GC = 2048, 2304, 2560, 3072
R_ZD, R_XBC, R_DT = 3584, 4096, 5120
REST_COLS = 5248
QKV_COLS = 3 * W_GRP
MIX_COL0 = R_UB
FUSE_ROWS = 2 * TILE_ROWS
N_MIX_PARAMS = 12


def _dot(a, b):
    return jnp.dot(a, b, preferred_element_type=F32)


def _dot_nt(a, b):
    return lax.dot_general(a, b, (((1,), (1,)), ((), ())), preferred_element_type=F32)


def _dot_tn(a, b):
    return lax.dot_general(a, b, (((0,), (0,)), ((), ())), preferred_element_type=F32)


def _silu(x):
    return x * (1.0 / (1.0 + jnp.exp(-x)))


def _alibi_slope2(h):
    return 2.0 ** (-8.0 * (h + 1) / HA) * LOG2E


def _inproj_kernel(x_ref, w_ref, *out_refs, with_bf16_kv):
    if with_bf16_kv:
        q_ref, k_ref, v_ref, kb_ref, vb_ref, r_ref = out_refs
    else:
        q_ref, k_ref, v_ref, r_ref = out_refs
    xb = x_ref[...].astype(BF16)
    q_ref[...] = (_dot(xb, w_ref[:, 0:W_GRP]) * Q_SCALE).astype(q_ref.dtype)
    k = _dot(xb, w_ref[:, W_GRP:2 * W_GRP])
    v = _dot(xb, w_ref[:, 2 * W_GRP:3 * W_GRP])
    for h in range(HA):
        k_ref[:, h, :] = k[:, h * LANES:(h + 1) * LANES]
        v_ref[:, h, :] = v[:, h * LANES:(h + 1) * LANES]
    if with_bf16_kv:
        kb_ref[...] = k.astype(BF16)
        vb_ref[...] = v.astype(BF16)
    r_ref[...] = _dot(xb, w_ref[:, QKV_COLS:])


def _inproj(x2, w_pad, layer, tm, *, ncols, with_bf16_kv):
    t, d = x2.shape
    row = lambda i: (i, 0)
    out_specs = [pl.BlockSpec((tm, W_GRP), row),
                 pl.BlockSpec((tm, HA, LANES), lambda i: (i, 0, 0)),
                 pl.BlockSpec((tm, HA, LANES), lambda i: (i, 0, 0))]
    out_shape = [jax.ShapeDtypeStruct((t, W_GRP), BF16 if with_bf16_kv else F32),
                 jax.ShapeDtypeStruct((t, HA, LANES), F32),
                 jax.ShapeDtypeStruct((t, HA, LANES), F32)]
    if with_bf16_kv:
        out_specs += [pl.BlockSpec((tm, W_GRP), row), pl.BlockSpec((tm, W_GRP), row)]
        out_shape += [jax.ShapeDtypeStruct((t, W_GRP), BF16), jax.ShapeDtypeStruct((t, W_GRP), BF16)]
    out_specs.append(pl.BlockSpec((tm, ncols - QKV_COLS), row))
    out_shape.append(jax.ShapeDtypeStruct((t, ncols - QKV_COLS), F32))
    return pl.pallas_call(
        functools.partial(_inproj_kernel, with_bf16_kv=with_bf16_kv),
        grid=(t // tm,),
        in_specs=[pl.BlockSpec((tm, d), row),
                  pl.BlockSpec((None, d, ncols), lambda i: (layer, 0, 0), pipeline_mode=pl.Buffered(1))],
        out_specs=out_specs,
        out_shape=out_shape,
        compiler_params=pltpu.CompilerParams(dimension_semantics=("arbitrary",),
                                             vmem_limit_bytes=VMEM_LIMIT),
        name="inproj",
    )(x2, w_pad)


def _split_maps(q):
    lane = lax.broadcasted_iota(jnp.int32, q.shape, 1)
    zero = jnp.zeros_like(q)
    q1 = jnp.where(lane < DA, q, zero)
    q2 = jnp.where(lane >= DA, q, zero)
    return jnp.concatenate([q1, q2], axis=0).astype(BF16)


def _lambda_value(lam_ref, lam_init):
    lv = lam_ref[...]
    a = jnp.sum(lv[0:1, :] * lv[1:2, :], axis=1, keepdims=True)
    b = jnp.sum(lv[2:3, :] * lv[3:4, :], axis=1, keepdims=True)
    return jnp.exp(a) - jnp.exp(b) + lam_init


def _finish_head(o, n, lam, lam_init, sg, ga):
    out = o[:n] - lam * o[n:]
    out = out * lax.rsqrt(jnp.mean(out * out, axis=1, keepdims=True) + LN_EPS) * sg * (1.0 - lam_init)
    return out * _silu(ga)


def _attn_prompt_kernel(lam_ref, q_ref, k_ref, v_ref, ga_ref, sg_ref, o_ref,
                        q2_s, s0_s, s1_s, p0_s, p1_s, m_s, a_s, l_s, acc_s, *, tq, kt, nq, lam_init):
    h = pl.program_id(1)
    slope = (jnp.where(h == 0, 2.0 ** -2, jnp.where(h == 1, 2.0 ** -4,
                       jnp.where(h == 2, 2.0 ** -6, 2.0 ** -8))) * LOG2E).astype(F32)
    s_bufs = (s0_s, s1_s)
    p_bufs = (p0_s, p1_s)
    tk = kt * tq
    lg_kt = int(math.log2(kt))
    last_j = lambda i: i >> lg_kt
    n_tiles = sum(last_j(i) + 1 for i in range(nq))
    for i in range(nq):
        q2_s[i] = _split_maps(q_ref[i * tq:(i + 1) * tq, :])
    m_s[...] = jnp.full(m_s.shape, NEG_INF, F32)
    l_s[...] = jnp.zeros(l_s.shape, F32)
    acc_s[...] = jnp.zeros(acc_s.shape, F32)
    col = lax.broadcasted_iota(jnp.int32, (1, tk), 1)

    def tile(ref, j):
        return ref[pl.ds(pl.multiple_of(j * tk, tk), tk), :]

    def qk(i, j, slot):
        s_bufs[slot][...] = _dot_nt(q2_s[i], tile(k_ref, j))

    def pv(i, j, slot):
        acc_s[i] = a_s[...] * acc_s[i] + _dot(p_bufs[slot][...], tile(v_ref, j))

    def softmax(i, j, slot, masked):
        shift = j * tk - i * tq
        bias = slope * (col + shift).astype(F32)
        if masked:
            r = lax.broadcasted_iota(jnp.int32, (2 * tq, LANES), 0) & (tq - 1)
            c = lax.broadcasted_iota(jnp.int32, (2 * tq, LANES), 1)
        sb = []
        for cb in range(tk // LANES):
            cols = slice(cb * LANES, (cb + 1) * LANES)
            x = s_bufs[slot][:, cols] + bias[:, cols]
            if masked:
                x = jnp.where(c + (cb * LANES + shift) <= r, x, NEG_INF)
            sb.append(x)
        m_old = m_s[i]
        m_new = jnp.maximum(m_old, jnp.max(functools.reduce(jnp.maximum, sb), axis=1, keepdims=True))
        alpha = jnp.exp2(m_old - m_new)
        pb = [jnp.exp2(x - m_new) for x in sb]
        l_s[i] = alpha * l_s[i] + functools.reduce(jnp.add, pb)
        for cb in range(tk // LANES):
            p_bufs[slot][:, cb * LANES:(cb + 1) * LANES] = pb[cb].astype(BF16)
        m_s[i] = m_new
        a_s[...] = alpha

    def next_tile(i, j):
        last_of_row = j == last_j(i)
        return jnp.where(last_of_row, i + 1, i), jnp.where(last_of_row, 0, j + 1)

    def prev_tile(i, j):
        first_of_row = j == 0
        return jnp.where(first_of_row, i - 1, i), jnp.where(first_of_row, last_j(i - 1), j - 1)

    def step(i, j, cur, masked):
        qk(*next_tile(i, j), 1 - cur)
        pv(*prev_tile(i, j), 1 - cur)
        softmax(i, j, cur, masked)

    second = (0, 1) if last_j(0) > 0 else (1, 0)
    qk(0, 0, 0)
    qk(*second, 1)
    softmax(0, 0, 0, last_j(0) == 0)

    def body(t, carry):
        i, j = carry
        for cur in (0, 1):
            for masked in (False, True):
                on_diagonal = (j == last_j(i)) if masked else (j != last_j(i))
                pl.when(((t & 1) == cur) & on_diagonal)(functools.partial(step, i, j, cur, masked))
        return next_tile(i, j)

    lax.fori_loop(1, n_tiles - 1, body, (jnp.int32(second[0]), jnp.int32(second[1])))

    cur = (n_tiles - 1) & 1
    final = (nq - 1, last_j(nq - 1))
    before_final = (nq - 1, final[1] - 1) if final[1] > 0 else (nq - 2, last_j(nq - 2))
    pv(*before_final, 1 - cur)
    softmax(*final, cur, True)
    pv(*final, cur)

    lam = _lambda_value(lam_ref, lam_init)
    for i in range(nq):
        rows = slice(i * tq, (i + 1) * tq)
        o = acc_s[i] / jnp.sum(l_s[i], axis=1, keepdims=True)
        o_ref[rows, :] = _finish_head(o, tq, lam, lam_init, sg_ref[...], ga_ref[rows, :]).astype(o_ref.dtype)


def _attn_prompt(q, kb, vb, rest, lamv, subln, *, batch, seq, tq, lam_init):
    kt = 2
    tk = kt * tq
    assert tq & (tq - 1) == 0 and tq % LANES == 0 and seq % tk == 0 and seq // tq >= 2
    t = batch * seq
    nq = seq // tq
    kern = functools.partial(_attn_prompt_kernel, tq=tq, kt=kt, nq=nq, lam_init=lam_init)
    head_block = pl.BlockSpec((seq, LANES), lambda b, h: (b, h))
    return pl.pallas_call(
        kern,
        grid=(batch, HA),
        in_specs=[pl.BlockSpec((4, DA), lambda b, h: (0, 0)),
                  head_block, head_block, head_block, head_block,
                  pl.BlockSpec((1, LANES), lambda b, h: (0, 0))],
        out_specs=head_block,
        out_shape=jax.ShapeDtypeStruct((t, W_GRP), BF16),
        scratch_shapes=[pltpu.VMEM((nq, 2 * tq, LANES), BF16),
                        pltpu.VMEM((2 * tq, tk), F32), pltpu.VMEM((2 * tq, tk), F32),
                        pltpu.VMEM((2 * tq, tk), BF16), pltpu.VMEM((2 * tq, tk), BF16),
                        pltpu.VMEM((nq, 2 * tq, LANES), F32), pltpu.VMEM((2 * tq, LANES), F32),
                        pltpu.VMEM((nq, 2 * tq, LANES), F32), pltpu.VMEM((nq, 2 * tq, LANES), F32)],
        compiler_params=pltpu.CompilerParams(
            dimension_semantics=("arbitrary", "arbitrary"), vmem_limit_bytes=VMEM_LIMIT),
        name="attn_prompt",
    )(lamv, q, kb, vb, rest, subln)


def _attn_decode_kernel(pt_ref, lam_ref, q_ref, kn_ref, vn_ref, ga_ref, sg_ref, *refs,
                        npg, past_len, lam_init):
    del pt_ref
    k_refs = refs[:npg]
    v_refs = refs[npg:2 * npg]
    o_ref = refs[2 * npg]
    m_s, l_s, acc_s = refs[2 * npg + 1:]
    j = pl.program_id(1)
    nj = pl.num_programs(1)
    nq = q_ref.shape[0]
    rows = 2 * nq
    blk = npg * PAGE

    @pl.when(j == 0)
    def _():
        m_s[...] = jnp.full(m_s.shape, NEG_INF, F32)
        l_s[...] = jnp.zeros(l_s.shape, F32)
        acc_s[...] = jnp.zeros(acc_s.shape, F32)

    q2 = [_split_maps(q_ref[:, h * LANES:(h + 1) * LANES]) for h in range(HA)]
    rowi = lax.broadcasted_iota(jnp.int32, (HA * rows, 1), 0)
    slope_col = jnp.full((HA * rows, 1), _alibi_slope2(HA - 1), F32)
    for h in range(HA - 2, -1, -1):
        slope_col = jnp.where(rowi < (h + 1) * rows, _alibi_slope2(h), slope_col)

    def update(s, v_of_head):
        m_old = m_s[...]
        m_new = jnp.maximum(m_old, jnp.max(s, axis=1, keepdims=True))
        alpha = jnp.exp2(m_old - m_new)
        p = jnp.exp2(s - m_new)
        l_new = alpha * l_s[...] + jnp.sum(p, axis=1, keepdims=True)
        p = p.astype(BF16)
        pv = jnp.concatenate([_dot(p[h * rows:(h + 1) * rows], v_of_head(h)) for h in range(HA)], axis=0)
        return m_new, l_new, alpha * acc_s[...] + pv

    def k_past(h):
        return jnp.concatenate([r[pl.ds(h, PAGE, stride=HA), :] for r in k_refs], axis=0).astype(BF16)

    def v_past(h):
        return jnp.concatenate([r[pl.ds(h, PAGE, stride=HA), :] for r in v_refs], axis=0).astype(BF16)

    col = lax.broadcasted_iota(jnp.int32, (1, blk), 1)
    s = jnp.concatenate([_dot_nt(q2[h], k_past(h)) for h in range(HA)], axis=0)
    s = s + slope_col * (col + (j * blk - past_len)).astype(F32)
    m_new, l_new, acc_new = update(s, v_past)
    m_s[...] = m_new
    l_s[...] = l_new
    acc_s[...] = acc_new

    @pl.when(j == nj - 1)
    def _():
        lam = _lambda_value(lam_ref, lam_init)
        zpad = jnp.zeros((LANES - nq, LANES), F32)
        kn = lambda h: jnp.concatenate([kn_ref[:, h, :], zpad], axis=0).astype(BF16)
        vn = lambda h: jnp.concatenate([vn_ref[:, h, :], zpad], axis=0).astype(BF16)
        r = lax.broadcasted_iota(jnp.int32, (HA * rows, LANES), 0) & (nq - 1)
        c = lax.broadcasted_iota(jnp.int32, (HA * rows, LANES), 1)
        sn = jnp.concatenate([_dot_nt(q2[h], kn(h)) for h in range(HA)], axis=0)
        sn = jnp.where(c <= r, sn + slope_col * c.astype(F32), NEG_INF)
        _, l_fin, acc_fin = update(sn, vn)
        o = acc_fin / l_fin
        for h in range(HA):
            hs = slice(h * LANES, (h + 1) * LANES)
            o_ref[:, hs] = _finish_head(o[h * rows:(h + 1) * rows], nq, lam, lam_init, sg_ref[...],
                                        ga_ref[:, hs])


def _attn_decode(pt_flat, q, kn, vn, rest, lamv, subln, cache_k, cache_v, *, layer, dec_b, dec_t,
                 n_pages, npg, lam_init):
    assert dec_t & (dec_t - 1) == 0
    past_len = n_pages * PAGE
    kern = functools.partial(_attn_decode_kernel, npg=npg, past_len=past_len, lam_init=lam_init)

    def page_spec(i):
        return pl.BlockSpec((None, None, PAGE * HA, LANES),
                            lambda b, j, pt: (layer, pt[b * n_pages + j * npg + i], 0, 0))

    row_spec = pl.BlockSpec((dec_t, W_GRP), lambda b, j, pt: (b, 0))
    new_spec = pl.BlockSpec((dec_t, HA, LANES), lambda b, j, pt: (b, 0, 0))
    grid_spec = pltpu.PrefetchScalarGridSpec(
        num_scalar_prefetch=1,
        grid=(dec_b, n_pages // npg),
        in_specs=[pl.BlockSpec((4, DA), lambda b, j, pt: (0, 0)),
                  row_spec, new_spec, new_spec, row_spec,
                  pl.BlockSpec((1, LANES), lambda b, j, pt: (0, 0))]
                 + [page_spec(i) for i in range(npg)] + [page_spec(i) for i in range(npg)],
        out_specs=row_spec,
        scratch_shapes=[pltpu.VMEM((HA * 2 * dec_t, 1), F32),
                        pltpu.VMEM((HA * 2 * dec_t, 1), F32),
                        pltpu.VMEM((HA * 2 * dec_t, LANES), F32)],
    )
    return pl.pallas_call(
        kern,
        grid_spec=grid_spec,
        out_shape=jax.ShapeDtypeStruct((dec_b * dec_t, W_GRP), F32),
        compiler_params=pltpu.CompilerParams(dimension_semantics=("arbitrary", "arbitrary"),
                                             vmem_limit_bytes=VMEM_LIMIT),
        name="attn_decode",
    )(pt_flat, lamv, q, kn, vn, rest, subln, *([cache_k] * npg), *([cache_v] * npg))


def _mixer_tile(rest_ref, is_first, ret0_ref, ssm0_ref, conv0_ref, glg_ref, glb_ref, ws_ref, bs_ref,
                rgg_ref, rgb_ref, cw_ref, cb_ref, dtb_ref, alog_ref, dskip_ref, sng_ref,
                out_ref, ret_ref, ssm_ref, convn_ref, vbn_ref, xpad_s, dl_s, wsm_s, dec_s,
                *, seg_len, n_seg, col0, after_init=None):
    def cols(start, width):
        return rest_ref[:, start - col0:start - col0 + width]

    L, G = seg_len, n_seg
    R = TILE_ROWS
    lg = int(math.log2(L))
    ret_log_g = [math.log(1.0 - 2.0 ** (-5.0 - h)) for h in range(HC)]

    tt = lax.broadcasted_iota(jnp.int32, (R, R), 0)
    ss = lax.broadcasted_iota(jnp.int32, (R, R), 1)
    same = (tt >> lg) == (ss >> lg)
    mask = same & (ss <= tt)

    def start_of_sequence():
        ret_ref[...] = ret0_ref[...]
        ssm_ref[...] = ssm0_ref[...]
        xpad_s[:, 8 - (CONV_W - 1):8, :] = conv0_ref[...]
        diff = jnp.maximum(tt - ss, 0).astype(F32)
        for g in range(GB):
            wsm_s[g] = jnp.where(mask, ws_ref[g], 0.0).astype(BF16)
        for h in range(HC):
            dec_s[h] = jnp.where(mask, jnp.exp(ret_log_g[h] * diff), 0.0)

    if is_first is not None:
        pl.when(is_first)(start_of_sequence)
    if after_init is not None:
        after_init()

    trow = lax.broadcasted_iota(jnp.int32, (R, 1), 0)
    tpos = (trow & (L - 1)).astype(F32)
    rseg = trow >> lg

    def over_segments(body, init):
        carry = init
        for j in range(G):
            carry = body(j, None if G == 1 else rseg == j, carry)
        return carry

    def rows_of(rm, x):
        return x if rm is None else jnp.where(rm, x, 0.0)

    vb = cols(R_VB, W_GRP)
    mu = jnp.mean(vb, axis=1, keepdims=True)
    vc_ = vb - mu
    var = jnp.mean(vc_ * vc_, axis=1, keepdims=True)
    vbn = vc_ * lax.rsqrt(var + LN_EPS) * glg_ref[...] + glb_ref[...]
    if vbn_ref is not None:
        vbn_ref[...] = vbn
    for g in range(GB):
        gs = slice(g * LANES, (g + 1) * LANES)
        mixed = _dot(wsm_s[g], vbn[:, gs].astype(BF16)) + bs_ref[:, g:g + 1]
        ub = cols(R_UB + g * LANES, LANES)
        gb = cols(R_GB + g * LANES, LANES)
        out_ref[:, gs] = (ub * mixed * _silu(gb)).astype(out_ref.dtype)

    for h in range(HC):
        log_g = ret_log_g[h]
        q = cols(R_QC + h * DKC, DKC).astype(BF16)
        k = cols(R_KC + h * DKC, DKC) * (DKC ** -0.5)
        v = cols(R_VC + h * DVC, DVC)
        scores = _dot_nt(q, k.astype(BF16)) * dec_s[h]
        o = _dot(scores.astype(BF16), v.astype(BF16))
        row_scale = jnp.exp(log_g * (tpos + 1.0))
        vw = (v * jnp.exp(log_g * (L - 1.0 - tpos))).astype(BF16)
        g_len = math.exp(log_g * L)

        def ret_body(j, rm, o_state, h=h, q=q, k=k, vw=vw, g_len=g_len):
            s0 = ret_ref[j, h]
            o_state = o_state + rows_of(rm, _dot(q, s0.astype(BF16)))
            ret_ref[j, h] = g_len * s0 + _dot_tn(rows_of(rm, k).astype(BF16), vw)
            return o_state

        o = o + over_segments(ret_body, jnp.zeros((R, DVC), F32)) * row_scale
        mu = jnp.mean(o, axis=1, keepdims=True)
        oc = o - mu
        var = jnp.mean(oc * oc, axis=1, keepdims=True)
        hs = slice(h * DVC, (h + 1) * DVC)
        oc = oc * lax.rsqrt(var + LN_EPS) * rgg_ref[:, hs] + rgb_ref[:, hs]
        gc = cols(R_GC + h * DVC, DVC)
        out_ref[:, W_GRP + h * DVC:W_GRP + (h + 1) * DVC] = (oc * _silu(gc)).astype(out_ref.dtype)

    x_cur = cols(R_XBC, CONV_DIM)
    xpad_s[:, 8:8 + L, :] = x_cur.reshape(G, L, CONV_DIM)
    conv = cb_ref[...] + cw_ref[CONV_W - 1:CONV_W, :] * x_cur
    for kk in range(1, CONV_W):
        shifted = xpad_s[:, 8 - kk:8 - kk + L, :].reshape(R, CONV_DIM)
        conv = conv + cw_ref[CONV_W - 1 - kk:CONV_W - kk, :] * shifted
    convn_ref[...] = xpad_s[:, L + 8 - (CONV_W - 1):L + 8, :]
    xpad_s[:, 0:8, :] = xpad_s[:, L:L + 8, :]
    conv = _silu(conv)
    xd = conv[:, :W_GRP]
    bd = conv[:, W_GRP:W_GRP + NG * NS]
    cd = conv[:, W_GRP + NG * NS:]

    zt = cols(R_DT, LANES) + dtb_ref[...]
    dt = jnp.maximum(zt, 0.0) + jnp.log1p(jnp.exp(-jnp.abs(zt)))
    dta = dt * (-jnp.exp(alog_ref[...]))
    cum = jnp.dot(mask.astype(F32), dta, precision=lax.Precision.HIGHEST, preferred_element_type=F32)
    if G == 1:
        cum_last = jnp.broadcast_to(cum[R - 1:R, :], (R, LANES))
    else:
        cum_last = jnp.dot(same.astype(F32), dta, precision=lax.Precision.HIGHEST,
                           preferred_element_type=F32)
    cum_t = cum.T
    dt_t = dt.T
    ecum = jnp.exp(cum)
    wlast = jnp.exp(cum_last - cum) * dt
    dl_s[...] = jnp.exp(cum_last)

    yd_parts = []
    hpg = HD // NG
    for grp in range(NG):
        bg = bd[:, grp * NS:(grp + 1) * NS].astype(BF16)
        cg = cd[:, grp * NS:(grp + 1) * NS].astype(BF16)
        cb = _dot_nt(cg, bg)
        heads = range(grp * hpg, (grp + 1) * hpg)
        x_hs = [xd[:, h * PD:(h + 1) * PD] for h in heads]
        xw = jnp.concatenate([x_h * wlast[:, h:h + 1] for h, x_h in zip(heads, x_hs)], axis=1)

        def ssd_body(j, rm, y_state, grp=grp, cg=cg, bg=bg, xw=xw):
            h0 = ssm_ref[j, grp * hpg:(grp + 1) * hpg].reshape(hpg * PD, NS)
            y_state = y_state + rows_of(rm, _dot_nt(cg, h0.astype(BF16)))
            upd = _dot_tn(rows_of(rm, xw).astype(BF16), bg)
            for hh in range(hpg):
                h = grp * hpg + hh
                rows = slice(hh * PD, (hh + 1) * PD)
                dec = dl_s[j * L:j * L + 1, h:h + 1]
                ssm_ref[j, h] = dec * h0[rows] + upd[rows]
            return y_state

        y_state = over_segments(ssd_body, jnp.zeros((R, hpg * PD), F32))
        for hh, (h, x_h) in enumerate(zip(heads, x_hs)):
            seg = cum[:, h:h + 1] - cum_t[h:h + 1, :]
            lmat = jnp.where(mask, jnp.exp(jnp.where(mask, seg, 0.0)), 0.0)
            scores = cb * lmat * dt_t[h:h + 1, :]
            y = _dot(scores.astype(BF16), x_h.astype(BF16))
            y = y + y_state[:, hh * PD:(hh + 1) * PD] * ecum[:, h:h + 1]
            yd_parts.append(y + dskip_ref[:, h * PD:(h + 1) * PD] * x_h)
    yd = jnp.concatenate(yd_parts, axis=1)
    z = yd * _silu(cols(R_ZD, W_GRP))
    out_ref[:, 2 * W_GRP:3 * W_GRP] = (
        z * lax.rsqrt(jnp.mean(z * z, axis=1, keepdims=True) + LN_EPS) * sng_ref[...]
    ).astype(out_ref.dtype)


def _mixer_kernel(rest_ref, ret0_ref, ssm0_ref, conv0_ref, *refs, seg_len, n_seg, emit_vbn):
    params = refs[:N_MIX_PARAMS]
    n_out = 5 if emit_vbn else 4
    outs = refs[N_MIX_PARAMS:N_MIX_PARAMS + n_out]
    scratch = refs[N_MIX_PARAMS + n_out:]
    out_ref, ret_ref, ssm_ref, convn_ref = outs[:4]
    _mixer_tile(rest_ref, pl.program_id(1) == 0, ret0_ref, ssm0_ref, conv0_ref, *params,
                out_ref, ret_ref, ssm_ref, convn_ref, outs[4] if emit_vbn else None, *scratch,
                seg_len=seg_len, n_seg=n_seg, col0=0)


def _fused_mixer_kernel(x_ref, w_ref, ret0_ref, ssm0_ref, conv0_ref, *refs, chunks_per_seq):
    params = refs[:N_MIX_PARAMS]
    out_ref, ret_ref, ssm_ref, convn_ref = refs[N_MIX_PARAMS:N_MIX_PARAMS + 4]
    rest0_s, rest1_s, xpad_s, dl_s, wsm_s, dec_s = refs[N_MIX_PARAMS + 4:]
    s = pl.program_id(0)
    tiles = FUSE_ROWS // TILE_ROWS

    @pl.when(s == 0)
    def _():
        rest1_s[...] = jnp.zeros(rest1_s.shape, F32)

    def run(cur):
        bufs = (rest0_s, rest1_s)

        def project():
            bufs[cur][...] = _dot(x_ref[...].astype(BF16), w_ref[...])

        for k in range(tiles):
            chunk0 = (s - 1) * tiles
            is_first = ((chunk0 & (chunks_per_seq - 1)) == 0) | (s == 0) if k == 0 else None
            rows = pl.ds(k * TILE_ROWS, TILE_ROWS)
            _mixer_tile(bufs[1 - cur].at[rows], is_first, ret0_ref, ssm0_ref, conv0_ref, *params,
                        out_ref.at[rows], ret_ref, ssm_ref, convn_ref, None,
                        xpad_s, dl_s, wsm_s, dec_s, seg_len=TILE_ROWS, n_seg=1, col0=MIX_COL0,
                        after_init=project if k == 0 else None)

    pl.when((s & 1) == 0)(lambda: run(0))
    pl.when((s & 1) == 1)(lambda: run(1))


def _mixer_param_specs(index_map_zero):
    shapes = [(1, W_GRP), (1, W_GRP), (GB, TILE_ROWS, TILE_ROWS), (TILE_ROWS, GB), (1, W_GRP),
              (1, W_GRP), (CONV_W, CONV_DIM), (1, CONV_DIM), (1, LANES), (1, LANES), (1, W_GRP),
              (1, W_GRP)]
    assert len(shapes) == N_MIX_PARAMS
    return [pl.BlockSpec(s, index_map_zero(len(s))) for s in shapes]


def _mixer_param_args(p):
    return (p["glg"], p["glb"], p["ws"], p["bs"], p["rgg"], p["rgb"], p["cw"], p["cb"], p["dtb"],
            p["alog"], p["dskip"], p["sng"])


def _fused_mixers(x2, w_mix, layer, ret0, ssm0, conv0, p, *, seq):
    t, d = x2.shape
    ncols = w_mix.shape[2]
    n = t // FUSE_ROWS
    steps_per_seq = seq // FUSE_ROWS
    chunks_per_seq = seq // TILE_ROWS
    assert chunks_per_seq & (chunks_per_seq - 1) == 0 and seq % FUSE_ROWS == 0
    n_seq = ret0.shape[0]
    seq_of = lambda s: jnp.maximum(s - 1, 0) // steps_per_seq
    in_specs = [pl.BlockSpec((FUSE_ROWS, d), lambda s: (jnp.minimum(s, n - 1), 0)),
                pl.BlockSpec((None, d, ncols), lambda s: (layer, 0, 0), pipeline_mode=pl.Buffered(1)),
                pl.BlockSpec((1, HC, DKC, DVC), lambda s: (seq_of(s), 0, 0, 0)),
                pl.BlockSpec((1, HD, PD, NS), lambda s: (seq_of(s), 0, 0, 0)),
                pl.BlockSpec((1, CONV_W - 1, CONV_DIM), lambda s: (seq_of(s), 0, 0))]
    in_specs += _mixer_param_specs(lambda nd: (lambda s, _nd=nd: (0,) * _nd))
    out_specs = [pl.BlockSpec((FUSE_ROWS, 3 * W_GRP), lambda s: (jnp.maximum(s - 1, 0), 0)),
                 pl.BlockSpec((1, HC, DKC, DVC), lambda s: (seq_of(s), 0, 0, 0)),
                 pl.BlockSpec((1, HD, PD, NS), lambda s: (seq_of(s), 0, 0, 0)),
                 pl.BlockSpec((1, CONV_W - 1, CONV_DIM), lambda s: (seq_of(s), 0, 0))]
    out_shape = [jax.ShapeDtypeStruct((t, 3 * W_GRP), BF16),
                 jax.ShapeDtypeStruct((n_seq, HC, DKC, DVC), F32),
                 jax.ShapeDtypeStruct((n_seq, HD, PD, NS), F32),
                 jax.ShapeDtypeStruct((n_seq, CONV_W - 1, CONV_DIM), F32)]
    return pl.pallas_call(
        functools.partial(_fused_mixer_kernel, chunks_per_seq=chunks_per_seq),
        grid=(n + 1,),
        in_specs=in_specs,
        out_specs=out_specs,
        out_shape=out_shape,
        scratch_shapes=[pltpu.VMEM((FUSE_ROWS, ncols), F32), pltpu.VMEM((FUSE_ROWS, ncols), F32),
                        pltpu.VMEM((1, TILE_ROWS + 8, CONV_DIM), F32),
                        pltpu.VMEM((TILE_ROWS, LANES), F32),
                        pltpu.VMEM((GB, TILE_ROWS, TILE_ROWS), BF16),
                        pltpu.VMEM((HC, TILE_ROWS, TILE_ROWS), F32)],
        compiler_params=pltpu.CompilerParams(dimension_semantics=("arbitrary",),
                                             vmem_limit_bytes=VMEM_LIMIT),
        name="fused_mixers",
    )(x2, w_mix, ret0, ssm0, conv0, *_mixer_param_args(p))


def _mixers(rest, ret0, ssm0, conv0, p, *, seg_len, n_chunks, emit_vbn):
    t = rest.shape[0]
    n_seq = ret0.shape[0]
    G = TILE_ROWS // seg_len
    n_grp = n_seq // G
    kern = functools.partial(_mixer_kernel, seg_len=seg_len, n_seg=G, emit_vbn=emit_vbn)

    in_specs = [pl.BlockSpec((TILE_ROWS, REST_COLS), lambda g, c: (g * n_chunks + c, 0)),
                pl.BlockSpec((G, HC, DKC, DVC), lambda g, c: (g, 0, 0, 0)),
                pl.BlockSpec((G, HD, PD, NS), lambda g, c: (g, 0, 0, 0)),
                pl.BlockSpec((G, CONV_W - 1, CONV_DIM), lambda g, c: (g, 0, 0))]
    in_specs += _mixer_param_specs(lambda nd: (lambda g, c, _nd=nd: (0,) * _nd))
    out_specs = [pl.BlockSpec((TILE_ROWS, 3 * W_GRP), lambda g, c: (g * n_chunks + c, 0)),
                 pl.BlockSpec((G, HC, DKC, DVC), lambda g, c: (g, 0, 0, 0)),
                 pl.BlockSpec((G, HD, PD, NS), lambda g, c: (g, 0, 0, 0)),
                 pl.BlockSpec((G, CONV_W - 1, CONV_DIM), lambda g, c: (g, 0, 0))]
    out_shape = [jax.ShapeDtypeStruct((t, 3 * W_GRP), BF16),
                 jax.ShapeDtypeStruct((n_seq, HC, DKC, DVC), F32),
                 jax.ShapeDtypeStruct((n_seq, HD, PD, NS), F32),
                 jax.ShapeDtypeStruct((n_seq, CONV_W - 1, CONV_DIM), F32)]
    if emit_vbn:
        out_specs.append(pl.BlockSpec((TILE_ROWS, W_GRP), lambda g, c: (g * n_chunks + c, 0)))
        out_shape.append(jax.ShapeDtypeStruct((t, W_GRP), F32))
    return pl.pallas_call(
        kern,
        grid=(n_grp, n_chunks),
        in_specs=in_specs,
        out_specs=out_specs,
        out_shape=out_shape,
        scratch_shapes=[pltpu.VMEM((G, seg_len + 8, CONV_DIM), F32),
                        pltpu.VMEM((TILE_ROWS, LANES), F32),
                        pltpu.VMEM((GB, TILE_ROWS, TILE_ROWS), BF16),
                        pltpu.VMEM((HC, TILE_ROWS, TILE_ROWS), F32)],
        compiler_params=pltpu.CompilerParams(dimension_semantics=("arbitrary", "arbitrary"),
                                             vmem_limit_bytes=VMEM_LIMIT),
        name="mixers",
    )(rest, ret0, ssm0, conv0, *_mixer_param_args(p))


def _outproj_kernel(a_ref, m_ref, x_ref, w_ref, g_ref, b_ref, y_ref, *, alpha):
    hproj = _dot(a_ref[...].astype(BF16), w_ref[0:W_GRP, :])
    hproj = hproj + _dot(m_ref[...], w_ref[W_GRP:, :])
    z = alpha * x_ref[...] + hproj
    mu = jnp.mean(z, axis=1, keepdims=True)
    zc = z - mu
    var = jnp.mean(zc * zc, axis=1, keepdims=True)
    y_ref[...] = zc * lax.rsqrt(var + LN_EPS) * g_ref[...] + b_ref[...]


def _outproj(out_a, out_bcd, x2, w_out, layer, ln_g, ln_b, tm, alpha):
    t, d = x2.shape
    kern = functools.partial(_outproj_kernel, alpha=alpha)
    return pl.pallas_call(
        kern,
        grid=(t // tm,),
        in_specs=[pl.BlockSpec((tm, W_GRP), lambda i: (i, 0)),
                  pl.BlockSpec((tm, 3 * W_GRP), lambda i: (i, 0)),
                  pl.BlockSpec((tm, d), lambda i: (i, 0)),
                  pl.BlockSpec((None, 4 * W_GRP, d), lambda i: (layer, 0, 0)),
                  pl.BlockSpec((1, d), lambda i: (0, 0)),
                  pl.BlockSpec((1, d), lambda i: (0, 0))],
        out_specs=pl.BlockSpec((tm, d), lambda i: (i, 0)),
        out_shape=jax.ShapeDtypeStruct((t, d), F32),
        compiler_params=pltpu.CompilerParams(dimension_semantics=("arbitrary",),
                                             vmem_limit_bytes=VMEM_LIMIT),
        name="outproj",
    )(out_a, out_bcd, x2, w_out, ln_g, ln_b)


def _layer_params(l, seg_len, gmlp_ln_g, gmlp_ln_b, gmlp_ws, gmlp_bs, ret_gn_g, ret_gn_b, conv_w,
                  conv_b, dt_bias, a_log, d_skip, ssd_norm_g):
    G = TILE_ROWS // seg_len
    pad8 = lambda v: jnp.pad(v.astype(F32), (0, LANES - HD)).reshape(1, LANES)
    return {
        "glg": gmlp_ln_g[l].reshape(1, W_GRP), "glb": gmlp_ln_b[l].reshape(1, W_GRP),
        "ws": jnp.tile(gmlp_ws[l][:, :seg_len, :seg_len], (1, G, G)),
        "bs": jnp.tile(gmlp_bs[l][:, :seg_len].T, (G, 1)),
        "rgg": ret_gn_g[l].reshape(1, W_GRP), "rgb": ret_gn_b[l].reshape(1, W_GRP),
        "cw": conv_w[l], "cb": conv_b[l].reshape(1, CONV_DIM),
        "dtb": pad8(dt_bias[l]), "alog": pad8(a_log[l]),
        "dskip": jnp.repeat(d_skip[l].astype(F32), PD).reshape(1, W_GRP),
        "sng": ssd_norm_g[l].reshape(1, W_GRP),
    }


def kernel(x_prompt, x_sample, cache_k, cache_v, page_table, state_ret, state_ssm, state_conv, w_in, w_out, ln_g, ln_b, lam_q1, lam_k1, lam_q2, lam_k2, subln_g, gmlp_ln_g, gmlp_ln_b, gmlp_ws, gmlp_bs, ret_gn_g, ret_gn_b, conv_w, conv_b, dt_bias, a_log, d_skip, ssd_norm_g):
    depth = w_in.shape[0]
    batch, seq, d_model = x_prompt.shape
    dec_b, dec_t, _ = x_sample.shape
    n_pages = page_table.shape[1]
    in_cols = w_in.shape[2]
    alpha = (2.0 * depth) ** 0.25
    assert in_cols == QKV_COLS + R_DT + HD and seq % TILE_ROWS == 0 and TILE_ROWS % dec_t == 0
    assert (dec_b * dec_t) % TILE_ROWS == 0

    w_in_p = jnp.pad(w_in, ((0, 0), (0, 0), (0, QKV_COLS + REST_COLS - in_cols))).astype(BF16)
    w_mix = w_in_p[:, :, QKV_COLS + MIX_COL0:]
    w_out_b = w_out.astype(BF16)
    ck = cache_k.reshape(cache_k.shape[0], cache_k.shape[1], PAGE * HA, LANES)
    cv = cache_v.reshape(cache_v.shape[0], cache_v.shape[1], PAGE * HA, LANES)
    pt_flat = page_table.reshape(-1).astype(jnp.int32)
    zeros_ret = jnp.zeros((batch, HC, DKC, DVC), F32)
    zeros_ssm = jnp.zeros((batch, HD, PD, NS), F32)
    zeros_conv = jnp.zeros((batch, CONV_W - 1, CONV_DIM), F32)
    mix_args = (gmlp_ln_g, gmlp_ln_b, gmlp_ws, gmlp_bs, ret_gn_g, ret_gn_b, conv_w, conv_b, dt_bias,
                a_log, d_skip, ssd_norm_g)

    tp = batch * seq
    tsm = dec_b * dec_t
    tq = min(256, seq)
    npg = next(n for n in (16, 8, 4, 2, 1) if n_pages % n == 0)
    yp = x_prompt.reshape(tp, d_model)
    ys = x_sample.reshape(tsm, d_model)
    outs = [[] for _ in range(11)]
    for l in range(depth):
        lam_init = 0.8 - 0.6 * math.exp(-0.3 * l)
        lamv = jnp.stack([lam_q1[l], lam_k1[l], lam_q2[l], lam_k2[l]]).astype(F32)
        subln = subln_g[l].reshape(1, LANES)
        lng = ln_g[l].reshape(1, d_model)
        lnb = ln_b[l].reshape(1, d_model)

        pp = _layer_params(l, TILE_ROWS, *mix_args)
        q, k, v, kb, vb, ga = _inproj(yp, w_in_p, l, 512, ncols=QKV_COLS + MIX_COL0, with_bf16_kv=True)
        oa = _attn_prompt(q, kb, vb, ga, lamv, subln, batch=batch, seq=seq, tq=tq, lam_init=lam_init)
        obcd, rp, hp, cp = _fused_mixers(yp, w_mix, l, zeros_ret, zeros_ssm, zeros_conv, pp, seq=seq)
        yp = _outproj(oa, obcd, yp, w_out_b, l, lng, lnb, 512, alpha)

        ps = _layer_params(l, dec_t, *mix_args)
        qs, ksn, vsn, rest_s = _inproj(ys, w_in_p, l, min(256, tsm), ncols=QKV_COLS + REST_COLS,
                                       with_bf16_kv=False)
        oas = _attn_decode(pt_flat, qs, ksn, vsn, rest_s, lamv, subln, ck, cv, layer=l, dec_b=dec_b,
                           dec_t=dec_t, n_pages=n_pages, npg=npg, lam_init=lam_init)
        obcd_s, rs, hs, cs, gs = _mixers(rest_s, state_ret[l], state_ssm[l], state_conv[l], ps,
                                         seg_len=dec_t, n_chunks=1, emit_vbn=True)
        ys = _outproj(oas, obcd_s, ys, w_out_b, l, lng, lnb, min(256, tsm), alpha)

        for lst, val in zip(outs, (k.reshape(batch, seq, HA, 2 * DA), v.reshape(batch, seq, HA, 2 * DA),
                                   ksn.reshape(dec_b, dec_t, HA, 2 * DA),
                                   vsn.reshape(dec_b, dec_t, HA, 2 * DA),
                                   rp, rs, hp, hs, cp, cs, gs.reshape(dec_b, dec_t, W_GRP))):
            lst.append(val)
    stacked = [jnp.stack(o) for o in outs]
    return (yp.reshape(batch, seq, d_model), ys.reshape(dec_b, dec_t, d_model), *stacked)
```
